```python
import math
import jax
import jax.numpy as jnp
from jax import lax
import numpy as np

D_MODEL = 1024
BATCH = 2
SEQ = 8192
DEPTH = 2

PLE_DIM = 256
N_A_LAYERS = DEPTH // 2
N_B_LAYERS = DEPTH - N_A_LAYERS
NORM_EPS = 1e-6
ROPE_THETA = 10000.0
SSM_EXPAND = 2
SSM_D_INNER = SSM_EXPAND * D_MODEL
SSM_HEAD_DIM = 64
SSM_HEADS = SSM_D_INNER // SSM_HEAD_DIM
SSM_GROUPS = 4
SSM_STATE = 128
SSM_CONV = 4
SSM_CHUNK = 256
SSM_CONV_DIM = SSM_D_INNER + 2 * SSM_GROUPS * SSM_STATE
SSM_IN_DIM = 2 * SSM_D_INNER + 2 * SSM_GROUPS * SSM_STATE + SSM_HEADS
ATT_HEAD_DIM = 64
ATT_HEADS = D_MODEL // ATT_HEAD_DIM
ATT_KV_GROUPS = 4
ATT_REP = ATT_HEADS // ATT_KV_GROUPS
N_BRANCH = 3
CMP_BLOCK = 32
CMP_STRIDE = 16
CMP_HIDDEN = 4 * ATT_HEAD_DIM
SEL_BLOCK = 64
SEL_TOPK = 16
WINDOW = 512
Q_BLOCK = 128
SEL_FORCE = 1000.0
N_EXPERTS = 32
TOP_K = 4
D_FF = D_MODEL
SWIGLU_LIMIT = 7.0
SWIGLU_ALPHA = 1.702
MOE_ROW_BLOCK = 128

kernel_name = 'hybrid_ssd_nsa_moe_yoco'


def rms_norm(x, g):
    xf = x.astype(jnp.float32)
    y = xf * lax.rsqrt(jnp.mean(xf * xf, axis=-1, keepdims=True) + NORM_EPS)
    return (y * g.astype(jnp.float32)).astype(x.dtype)


def rope(x, pos):
    dh = x.shape[-1]
    inv = 1.0 / (ROPE_THETA ** (jnp.arange(0, dh, 2, dtype=jnp.float32) / dh))
    ang = pos.astype(jnp.float32)[..., None] * inv
    c, s = jnp.cos(ang), jnp.sin(ang)
    xf = x.astype(jnp.float32)
    x1, x2 = xf[..., : dh // 2], xf[..., dh // 2:]
    return jnp.concatenate([x1 * c - x2 * s, x2 * c + x1 * s], axis=-1).astype(x.dtype)


def masked_softmax(s, mask):
    s = jnp.where(mask, s, -jnp.inf)
    m = jnp.max(s, axis=-1, keepdims=True)
    m = jnp.where(jnp.isfinite(m), m, 0.0)
    e = jnp.exp(s - m)
    den = jnp.sum(e, axis=-1, keepdims=True)
    return e / jnp.where(den > 0, den, 1.0)


def causal_dwconv(x, w, b):
    k = w.shape[1]
    y = lax.conv_general_dilated(
        x, w.T[:, None, :].astype(x.dtype), window_strides=(1,), padding=[(k - 1, 0)],
        dimension_numbers=('NWC', 'WIO', 'NWC'), feature_group_count=x.shape[-1])
    return y + b.astype(x.dtype)


def ssd_scan(X, A, Bm, Cm, chunk):
    b, S, H, P = X.shape
    G, N = Bm.shape[2], Bm.shape[3]
    R = H // G
    c = S // chunk
    X = X.reshape(b, c, chunk, G, R, P)
    A = A.reshape(b, c, chunk, G, R).transpose(0, 3, 4, 1, 2)
    Bm = Bm.reshape(b, c, chunk, G, N)
    Cm = Cm.reshape(b, c, chunk, G, N)
    A_cs = jnp.cumsum(A, axis=-1)
    causal = jnp.tril(jnp.ones((chunk, chunk), dtype=bool))
    L = jnp.exp(jnp.where(causal, A_cs[..., :, None] - A_cs[..., None, :], -jnp.inf))
    CB = jnp.einsum('bclgn,bcsgn->bgcls', Cm, Bm)
    y_diag = jnp.einsum('bgcls,bgrcls,bcsgrp->bclgrp', CB, L, X)
    decay = jnp.exp(A_cs[..., -1:] - A_cs)
    states = jnp.einsum('bclgn,bgrcl,bclgrp->bcgrpn', Bm, decay, X)
    chunk_decay = jnp.exp(A_cs[..., -1])

    def step(h, inp):
        st, dec = inp
        return dec[..., None, None] * h + st, h

    _, prev = lax.scan(step, jnp.zeros((b, G, R, P, N), X.dtype),
                       (jnp.moveaxis(states, 1, 0), jnp.moveaxis(chunk_decay, -1, 0)))
    prev = jnp.moveaxis(prev, 0, 1)
    y_off = jnp.einsum('bclgn,bcgrpn,bgrcl->bclgrp', Cm, prev, jnp.exp(A_cs))
    return (y_diag + y_off).reshape(b, S, H, P)


def mamba2_mixer(u, w_in, conv_w, conv_b, dt_bias, a_log, d_skip, g_norm, w_out):
    b, S, _ = u.shape
    f32 = jnp.float32
    zxbcdt = u @ w_in
    z = zxbcdt[..., :SSM_D_INNER]
    xbc = zxbcdt[..., SSM_D_INNER:SSM_D_INNER + SSM_CONV_DIM]
    dt = zxbcdt[..., SSM_D_INNER + SSM_CONV_DIM:]
    xbc = jax.nn.silu(causal_dwconv(xbc, conv_w, conv_b))
    gn = SSM_GROUPS * SSM_STATE
    xs = xbc[..., :SSM_D_INNER].reshape(b, S, SSM_HEADS, SSM_HEAD_DIM).astype(f32)
    Bm = xbc[..., SSM_D_INNER:SSM_D_INNER + gn].reshape(b, S, SSM_GROUPS, SSM_STATE).astype(f32)
    Cm = xbc[..., SSM_D_INNER + gn:].reshape(b, S, SSM_GROUPS, SSM_STATE).astype(f32)
    dt = jax.nn.softplus(dt.astype(f32) + dt_bias.astype(f32))
    A = -jnp.exp(a_log.astype(f32))
    y = ssd_scan(xs * dt[..., None], dt * A, Bm, Cm, math.gcd(S, SSM_CHUNK))
    y = y + d_skip.astype(f32)[:, None] * xs
    y = y.reshape(b, S, SSM_D_INNER) * jax.nn.silu(z.astype(f32))
    y = y.reshape(b, S, SSM_GROUPS, SSM_D_INNER // SSM_GROUPS)
    y = y * lax.rsqrt(jnp.mean(y * y, axis=-1, keepdims=True) + NORM_EPS)
    y = (y.reshape(b, S, SSM_D_INNER) * g_norm.astype(f32)).astype(u.dtype)
    return y @ w_out


def cmp_to_sel_weights(seq):
    n_cmp = (seq - CMP_BLOCK) // CMP_STRIDE + 1
    n_sel = seq // SEL_BLOCK
    cs = np.arange(n_cmp) * CMP_STRIDE
    ss = np.arange(n_sel) * SEL_BLOCK
    ov = np.minimum(cs[:, None] + CMP_BLOCK, ss[None, :] + SEL_BLOCK) - np.maximum(cs[:, None], ss[None, :])
    return jnp.asarray(np.clip(ov, 0, None).astype(np.float32) / CMP_BLOCK)


def nsa_shared_kv(r, kv_g, kv_w, cmp_pe, cmp_w1, cmp_w2, k_norm):
    b, s, _ = r.shape
    xn = rms_norm(r, kv_g)
    kv = (xn @ kv_w).reshape(b, s, 6, ATT_KV_GROUPS, ATT_HEAD_DIM)
    k_c, v_c, k_s, v_s, k_w, v_w = (kv[:, :, n] for n in range(6))
    n_cmp = (s - CMP_BLOCK) // CMP_STRIDE + 1
    blk_idx = jnp.arange(n_cmp)[:, None] * CMP_STRIDE + jnp.arange(CMP_BLOCK)[None, :]
    cmp_end = blk_idx[:, -1]

    def compress(u, pe, w1, w2):
        blocks = u[:, blk_idx] + pe[:, None, :].astype(u.dtype)
        blocks = blocks.transpose(0, 1, 3, 2, 4).reshape(b, n_cmp, ATT_KV_GROUPS, CMP_BLOCK * ATT_HEAD_DIM)
        return jax.nn.silu(blocks @ w1) @ w2

    kc = rope(rms_norm(compress(k_c, cmp_pe[0], cmp_w1[0], cmp_w2[0]), k_norm[0]), cmp_end[:, None])
    vc = compress(v_c, cmp_pe[1], cmp_w1[1], cmp_w2[1])
    pos = jnp.arange(s)[:, None]
    ks = rope(rms_norm(k_s, k_norm[1]), pos)
    kw = rope(rms_norm(k_w, k_norm[2]), pos)
    t = lambda a: a.transpose(0, 2, 1, 3)
    return (t(kc), t(vc), t(ks), t(v_s), t(kw), t(v_w))


def nsa_mixer(xn, q_w, q_norm, o_w, kc, vc, ks, vs, kw, vw):
    b, s, _ = xn.shape
    H, G, R, dh = ATT_HEADS, ATT_KV_GROUPS, ATT_REP, ATT_HEAD_DIM
    f32 = jnp.float32
    qg = xn @ q_w
    q = rms_norm(qg[..., :H * dh].reshape(b, s, H, dh), q_norm)
    q = rope(q, jnp.arange(s)[:, None])
    q = q.reshape(b, s, G, R, dh).transpose(0, 2, 3, 1, 4)
    gates = jax.nn.sigmoid(qg[..., H * dh:].astype(f32)).reshape(b, s, G, R, N_BRANCH).transpose(0, 2, 3, 1, 4)
    n_cmp = kc.shape[2]
    cmp_end = jnp.arange(n_cmp) * CMP_STRIDE + (CMP_BLOCK - 1)
    n_sel = s // SEL_BLOCK
    k_top = min(SEL_TOPK, n_sel)
    sel_w = cmp_to_sel_weights(s)
    ks_b = ks.reshape(b, G, n_sel, SEL_BLOCK, dh)
    vs_b = vs.reshape(b, G, n_sel, SEL_BLOCK, dh)
    pad = ((0, 0), (0, 0), (WINDOW, 0), (0, 0))
    kw_p = jnp.pad(kw, pad)
    vw_p = jnp.pad(vw, pad)
    bi = jnp.arange(b)[:, None, None, None]
    gi = jnp.arange(G)[None, :, None, None]
    scale = dh ** -0.5
    sel_off = jnp.arange(SEL_BLOCK)
    win_off = jnp.arange(Q_BLOCK + WINDOW) - WINDOW
    j = jnp.arange(n_sel)

    def query_block(i):
        t0 = i * Q_BLOCK
        t = t0 + jnp.arange(Q_BLOCK)
        qb = lax.dynamic_slice_in_dim(q, t0, Q_BLOCK, axis=3)
        gb = lax.dynamic_slice_in_dim(gates, t0, Q_BLOCK, axis=3)
        s_c = jnp.einsum('bgrqd,bgkd->bgrqk', qb, kc, preferred_element_type=f32) * scale
        p_c = masked_softmax(s_c, cmp_end[None, :] <= t[:, None])
        o_c = jnp.einsum('bgrqk,bgkd->bgrqd', p_c.astype(vc.dtype), vc)
        imp = jnp.einsum('bgrqk,kj->bgqj', p_c, sel_w)
        cur = t // SEL_BLOCK
        valid = j[None, :] <= cur[:, None]
        forced = (j[None, :] == 0) | (j[None, :] == cur[:, None]) | (j[None, :] == cur[:, None] - 1)
        score = jnp.where(valid, imp + jnp.where(forced, SEL_FORCE, 0.0), -1.0)
        _, idx = lax.top_k(score, k_top)
        blk_ok = idx <= cur[None, None, :, None]
        k_sel = ks_b[bi, gi, idx]
        v_sel = vs_b[bi, gi, idx]
        kpos = idx[..., None] * SEL_BLOCK + sel_off
        m_s = ((kpos <= t[None, None, :, None, None]) & blk_ok[..., None])[:, :, None]
        m_s = m_s.reshape(b, G, 1, Q_BLOCK, k_top * SEL_BLOCK)
        s_s = jnp.einsum('bgrqd,bgqkld->bgrqkl', qb, k_sel, preferred_element_type=f32) * scale
        p_s = masked_softmax(s_s.reshape(b, G, R, Q_BLOCK, k_top * SEL_BLOCK), m_s)
        p_s = p_s.reshape(b, G, R, Q_BLOCK, k_top, SEL_BLOCK).astype(v_sel.dtype)
        o_s = jnp.einsum('bgrqkl,bgqkld->bgrqd', p_s, v_sel)
        k_win = lax.dynamic_slice_in_dim(kw_p, t0, Q_BLOCK + WINDOW, axis=2)
        v_win = lax.dynamic_slice_in_dim(vw_p, t0, Q_BLOCK + WINDOW, axis=2)
        kp = t0 + win_off
        m_w = (kp[None, :] <= t[:, None]) & (kp[None, :] > t[:, None] - WINDOW) & (kp[None, :] >= 0)
        s_w = jnp.einsum('bgrqd,bgkd->bgrqk', qb, k_win, preferred_element_type=f32) * scale
        p_w = masked_softmax(s_w, m_w)
        o_win = jnp.einsum('bgrqk,bgkd->bgrqd', p_w.astype(v_win.dtype), v_win)
        o = gb[..., 0:1] * o_c + gb[..., 1:2] * o_s + gb[..., 2:3] * o_win
        return o.astype(xn.dtype).transpose(0, 3, 1, 2, 4).reshape(b, Q_BLOCK, H * dh)

    out = lax.map(query_block, jnp.arange(s // Q_BLOCK))
    out = out.transpose(1, 0, 2, 3).reshape(b, s, H * dh)
    return out @ o_w


def moe_ffn(xn, r_w, r_b, w_gu, b_gu, w_dn, b_dn):
    b, s, d = xn.shape
    n_tok = b * s
    nk = n_tok * TOP_K
    xt = xn.reshape(n_tok, d)
    logits = (xt @ r_w).astype(jnp.float32) + r_b.astype(jnp.float32)
    top_v, top_i = lax.top_k(logits, TOP_K)
    wts = jax.nn.softmax(top_v, axis=-1)
    flat_e = top_i.reshape(-1)
    flat_w = wts.reshape(-1)
    flat_tok = jnp.arange(nk) // TOP_K
    order = jnp.argsort(flat_e)
    se, stok, sw = flat_e[order], flat_tok[order], flat_w[order]
    counts = jnp.zeros((N_EXPERTS,), jnp.int32).at[flat_e].add(1)
    padded = ((counts + MOE_ROW_BLOCK - 1) // MOE_ROW_BLOCK) * MOE_ROW_BLOCK
    pad_end = jnp.cumsum(padded)
    pad_start = pad_end - padded
    grp_start = jnp.cumsum(counts) - counts
    dest = pad_start[se] + jnp.arange(nk) - grp_start[se]
    n_rows = nk + N_EXPERTS * MOE_ROW_BLOCK
    row_tok = jnp.zeros((n_rows,), jnp.int32).at[dest].set(stok)
    row_w = jnp.zeros((n_rows,), jnp.float32).at[dest].set(sw)
    n_blk = n_rows // MOE_ROW_BLOCK
    blk_e = jnp.minimum(jnp.searchsorted(pad_end, jnp.arange(n_blk) * MOE_ROW_BLOCK, side='right'), N_EXPERTS - 1)

    def expert_block(args):
        bidx, e = args
        tok = lax.dynamic_slice_in_dim(row_tok, bidx * MOE_ROW_BLOCK, MOE_ROW_BLOCK)
        h = xt[tok] @ w_gu[e] + b_gu[e]
        gate = jnp.minimum(h[:, 0::2], SWIGLU_LIMIT)
        up = jnp.clip(h[:, 1::2], -SWIGLU_LIMIT, SWIGLU_LIMIT)
        glu = gate * jax.nn.sigmoid(SWIGLU_ALPHA * gate)
        return ((up + 1.0) * glu) @ w_dn[e] + b_dn[e]

    ys = lax.map(expert_block, (jnp.arange(n_blk), blk_e)).reshape(n_rows, d)
    out = jnp.zeros((n_tok, d), xt.dtype).at[row_tok].add(ys * row_w[:, None].astype(ys.dtype))
    return out.reshape(b, s, d)


def per_layer_embed(r, p_i, w_ple, g_ple, w_gate):
    return (p_i @ w_ple) * jax.nn.sigmoid(rms_norm(r, g_ple) @ w_gate)


def setup_inputs(seed: int = 0) -> dict:
    key = jax.random.key(seed)
    ks = iter(jax.random.split(key, 48))
    f32 = jnp.float32
    D, NA, NB = D_MODEL, N_A_LAYERS, N_B_LAYERS
    dh, G, H = ATT_HEAD_DIM, ATT_KV_GROUPS, ATT_HEADS

    def nrm(shape, fan_in):
        return jax.random.normal(next(ks), shape, f32) * (fan_in ** -0.5)

    def gain(shape):
        return 1.0 + 0.02 * jax.random.normal(next(ks), shape, f32)

    def small(shape, scale=0.01):
        return scale * jax.random.normal(next(ks), shape, f32)

    x = jax.random.normal(next(ks), (BATCH, SEQ, D), f32)
    p = jax.random.normal(next(ks), (DEPTH, BATCH, SEQ, PLE_DIM), f32)
    g_mix = gain((DEPTH, D))
    g_ffn = gain((DEPTH, D))
    m_w_in = nrm((NA, D, SSM_IN_DIM), D)
    m_conv_w = nrm((NA, SSM_CONV_DIM, SSM_CONV), SSM_CONV)
    m_conv_b = small((NA, SSM_CONV_DIM))
    dt0 = jnp.exp(jax.random.uniform(next(ks), (NA, SSM_HEADS), f32, math.log(1e-3), math.log(1e-1)))
    m_dt_bias = dt0 + jnp.log(-jnp.expm1(-dt0))
    m_a_log = jnp.log(jax.random.uniform(next(ks), (NA, SSM_HEADS), f32, 1.0, 16.0))
    m_d = gain((NA, SSM_HEADS))
    m_g_norm = gain((NA, SSM_D_INNER))
    m_w_out = nrm((NA, SSM_D_INNER, D), SSM_D_INNER)
    kv_g = gain((D,))
    kv_w = nrm((D, 6 * G * dh), D)
    cmp_pe = small((2, CMP_BLOCK, dh), 0.1)
    cmp_w1 = nrm((2, CMP_BLOCK * dh, CMP_HIDDEN), CMP_BLOCK * dh)
    cmp_w2 = nrm((2, CMP_HIDDEN, dh), CMP_HIDDEN)
    k_norm = gain((N_BRANCH, dh))
    q_w = nrm((NB, D, H * dh + N_BRANCH * H), D)
    q_norm = gain((NB, dh))
    o_w = nrm((NB, H * dh, D), H * dh)
    r_w = nrm((DEPTH, D, N_EXPERTS), D)
    r_b = small((DEPTH, N_EXPERTS))
    e_w_gu = nrm((DEPTH, N_EXPERTS, D, 2 * D_FF), D)
    e_b_gu = small((DEPTH, N_EXPERTS, 2 * D_FF))
    e_w_dn = nrm((DEPTH, N_EXPERTS, D_FF, D), D_FF)
    e_b_dn = small((DEPTH, N_EXPERTS, D))
    ple_w = nrm((DEPTH, PLE_DIM, D), PLE_DIM)
    ple_g = gain((DEPTH, D))
    ple_gate_w = nrm((DEPTH, D, D), D)
    return {'x': x, 'p': p, 'g_mix': g_mix, 'g_ffn': g_ffn,
            'm_w_in': m_w_in, 'm_conv_w': m_conv_w, 'm_conv_b': m_conv_b, 'm_dt_bias': m_dt_bias,
            'm_a_log': m_a_log, 'm_d': m_d, 'm_g_norm': m_g_norm, 'm_w_out': m_w_out,
            'kv_g': kv_g, 'kv_w': kv_w, 'cmp_pe': cmp_pe, 'cmp_w1': cmp_w1, 'cmp_w2': cmp_w2, 'k_norm': k_norm,
            'q_w': q_w, 'q_norm': q_norm, 'o_w': o_w,
            'r_w': r_w, 'r_b': r_b, 'e_w_gu': e_w_gu, 'e_b_gu': e_b_gu, 'e_w_dn': e_w_dn, 'e_b_dn': e_b_dn,
            'ple_w': ple_w, 'ple_g': ple_g, 'ple_gate_w': ple_gate_w}


def reference(x, p, g_mix, g_ffn, m_w_in, m_conv_w, m_conv_b, m_dt_bias, m_a_log, m_d, m_g_norm, m_w_out,
              kv_g, kv_w, cmp_pe, cmp_w1, cmp_w2, k_norm, q_w, q_norm, o_w,
              r_w, r_b, e_w_gu, e_b_gu, e_w_dn, e_b_dn, ple_w, ple_g, ple_gate_w):
    r = x
    shared = None
    for i in range(DEPTH):
        if i == N_A_LAYERS:
            shared = nsa_shared_kv(r, kv_g, kv_w, cmp_pe, cmp_w1, cmp_w2, k_norm)
        h = rms_norm(r, g_mix[i])
        if i < N_A_LAYERS:
            r = r + mamba2_mixer(h, m_w_in[i], m_conv_w[i], m_conv_b[i], m_dt_bias[i], m_a_log[i],
                                 m_d[i], m_g_norm[i], m_w_out[i])
        else:
            jb = i - N_A_LAYERS
            r = r + nsa_mixer(h, q_w[jb], q_norm[jb], o_w[jb], *shared)
        r = r + moe_ffn(rms_norm(r, g_ffn[i]), r_w[i], r_b[i], e_w_gu[i], e_b_gu[i], e_w_dn[i], e_b_dn[i])
        r = r + per_layer_embed(r, p[i], ple_w[i], ple_g[i], ple_gate_w[i])
    return r
```

```python
import functools
import math

import numpy as np
import jax
import jax.numpy as jnp
from jax import lax
from jax.experimental import pallas as pl
from jax.experimental.pallas import tpu as pltpu

F32 = jnp.float32
BF16 = jnp.bfloat16

NORM_EPS = 1e-6
ROPE_THETA = 10000.0
SSM_HEADS = 32
SSM_HEAD_DIM = 64
SSM_GROUPS = 4
SSM_STATE = 128
SSM_CONV = 4
SSM_CHUNK = 256
SSM_D_INNER = SSM_HEADS * SSM_HEAD_DIM
SSM_GN = SSM_GROUPS * SSM_STATE
SSM_CONV_DIM = SSM_D_INNER + 2 * SSM_GN
SSM_GROUP_W = SSM_D_INNER // SSM_GROUPS
ATT_HEAD_DIM = 64
ATT_KV_GROUPS = 4
ATT_REP = 4
ATT_HEADS = ATT_KV_GROUPS * ATT_REP
N_BRANCH = 3
CMP_BLOCK = 32
CMP_STRIDE = 16
SEL_BLOCK = 64
SEL_TOPK = 16
WINDOW = 512
Q_BLOCK = 128
SEL_FORCE = 1000.0
SEL_CHUNK = 512
N_EXPERTS = 32
TOP_K = 4
SWIGLU_LIMIT = 7.0
SWIGLU_ALPHA = 1.702
EXPERT_ROWS = 256

LANES = 128
SUBLANES = 8
MXU_DIM = 256
VMEM_LIMIT_BYTES = 56 * 1024 * 1024
NEG = -1e30


def _params(*sem):
    return pltpu.CompilerParams(dimension_semantics=sem, vmem_limit_bytes=VMEM_LIMIT_BYTES)


def _dot(a, b):
    return jnp.dot(a, b, preferred_element_type=F32)


def _dot_nt(a, b):
    return lax.dot_general(a, b, (((1,), (1,)), ((), ())), preferred_element_type=F32)


def _split2(x):
    hi = x.astype(BF16)
    lo = (x - hi.astype(F32)).astype(BF16)
    return hi, lo


def _split3(x):
    hi = x.astype(BF16)
    r = x - hi.astype(F32)
    mid = r.astype(BF16)
    lo = (r - mid.astype(F32)).astype(BF16)
    return hi, mid, lo


def _dot_parts(parts, m):
    out = _dot(parts[0], m)
    for p in parts[1:]:
        out = out + _dot(p, m)
    return out


def _rms_hat(x):
    return x * lax.rsqrt(jnp.mean(x * x, axis=-1, keepdims=True) + NORM_EPS)


def _silu(x):
    return x * jax.nn.sigmoid(x)


def _full(shape):
    nd = len(shape)
    return pl.BlockSpec(shape, lambda *_: (0,) * nd)


def _inproj_kernel(x_ref, g_ref, wz_ref, wx_ref, wdh_ref, wdl_ref, z_ref, xbc_ref, dt_ref):
    h = _rms_hat(x_ref[...]) * g_ref[...]
    hb, hl = _split2(h)
    z_ref[...] = _dot(hb, wz_ref[...])
    xbc_ref[...] = _dot(hb, wx_ref[...])
    dt_ref[...] = _dot(hb, wdh_ref[...]) + _dot(hl, wdh_ref[...]) + _dot(hb, wdl_ref[...])


def _inproj(x, g, w_in):
    m, d = x.shape
    tm = 256
    wz = w_in[:, :SSM_D_INNER].astype(BF16)
    wx = w_in[:, SSM_D_INNER:SSM_D_INNER + SSM_CONV_DIM].astype(BF16)
    wd = jnp.pad(w_in[:, SSM_D_INNER + SSM_CONV_DIM:], ((0, 0), (0, LANES - SSM_HEADS)))
    wdh, wdl = _split2(wd)
    row = lambda n: pl.BlockSpec((tm, n), lambda i: (i, 0))
    return pl.pallas_call(
        _inproj_kernel,
        grid=(m // tm,),
        in_specs=[row(d), _full((1, d)), _full((d, SSM_D_INNER)), _full((d, SSM_CONV_DIM)),
                  _full((d, LANES)), _full((d, LANES))],
        out_specs=[row(SSM_D_INNER), row(SSM_CONV_DIM), row(LANES)],
        out_shape=[jax.ShapeDtypeStruct((m, SSM_D_INNER), F32),
                   jax.ShapeDtypeStruct((m, SSM_CONV_DIM), F32),
                   jax.ShapeDtypeStruct((m, LANES), F32)],
        compiler_params=_params("parallel"),
        name="ssm_inproj",
    )(x, g.reshape(1, d), wz, wx, wdh, wdl)


def _ssd_kernel(z_ref, xbc_ref, dt_ref, x_ref, cw_ref, cb_ref, dtb_ref, alog_ref, dsk_ref, gn_ref,
                wout_ref, exp_ref, o_ref, state_ref, ext_ref, y_ref):
    L = SSM_CHUNK
    c = pl.program_id(1)

    @pl.when(c == 0)
    def _():
        state_ref[...] = jnp.zeros_like(state_ref)
        ext_ref[0:SUBLANES, :] = jnp.zeros((SUBLANES, SSM_CONV_DIM), F32)

    ext_ref[SUBLANES:SUBLANES + L, :] = xbc_ref[...]
    acc = jnp.broadcast_to(cb_ref[...], (L, SSM_CONV_DIM))
    for k in range(SSM_CONV):
        off = SUBLANES - (SSM_CONV - 1) + k
        acc = acc + cw_ref[k:k + 1, :] * ext_ref[off:off + L, :]
    ext_ref[0:SUBLANES, :] = ext_ref[L:L + SUBLANES, :]
    xc = _silu(acc)
    xs = xc[:, :SSM_D_INNER]
    bm = xc[:, SSM_D_INNER:SSM_D_INNER + SSM_GN]
    cm = xc[:, SSM_D_INNER + SSM_GN:]

    dtr = dt_ref[...] + dtb_ref[...]
    dt = jnp.maximum(dtr, 0.0) + jnp.log(1.0 + jnp.exp(-jnp.abs(dtr)))
    a_neg = -jnp.exp(alog_ref[...])
    da = dt * a_neg
    row = lax.broadcasted_iota(jnp.int32, (L, L), 0)
    col = lax.broadcasted_iota(jnp.int32, (L, L), 1)
    causal = row >= col
    tri = jnp.where(causal, 1.0, 0.0).astype(BF16)
    da_hi, da_mid, da_lo = _split3(da)
    acs = _dot(tri, da_hi) + _dot(tri, da_mid) + _dot(tri, da_lo)
    a_last = acs[L - 1:L, :]
    d_end = jnp.exp(a_last - acs)
    d_start = jnp.exp(acs)
    cdec = jnp.broadcast_to(jnp.exp(a_last), (SUBLANES, LANES))

    ex = exp_ref[...]
    dt_e = _dot(dt.astype(BF16), ex)
    d_end_e = _dot(d_end.astype(BF16), ex)
    d_start_e = _dot_parts(_split2(d_start), ex)
    cdec_e = _dot_parts(_split2(cdec), ex)[0:1, :]

    xdt = xs * dt_e
    xb = xdt.astype(BF16)
    xd = (xdt * d_end_e).astype(BF16)
    acs_t = acs.T
    bm_t = bm.T
    bmb = bm.astype(BF16)
    cmb = cm.astype(BF16)

    for g in range(SSM_GROUPS):
        gs = slice(g * SSM_STATE, (g + 1) * SSM_STATE)
        ws = slice(g * SSM_GROUP_W, (g + 1) * SSM_GROUP_W)
        cb = _dot_nt(cmb[:, gs], bmb[:, gs])
        for r in range(SSM_HEADS // SSM_GROUPS):
            h = g * (SSM_HEADS // SSM_GROUPS) + r
            hs = slice(h * SSM_HEAD_DIM, (h + 1) * SSM_HEAD_DIM)
            diff = acs[:, h:h + 1] - acs_t[h:h + 1, :]
            lm = jnp.exp(jnp.where(causal, diff, -jnp.inf))
            y_ref[:, hs] = _dot((cb * lm).astype(BF16), xb[:, hs])
        st = state_ref[g]
        y_ref[:, ws] = y_ref[:, ws] + _dot(cmb[:, gs], st.astype(BF16)) * d_start_e[:, ws]
        state_ref[g] = cdec_e[:, ws] * st + _dot(bm_t[gs, :].astype(BF16), xd[:, ws])

    zz = z_ref[...]
    y = (y_ref[...] + dsk_ref[...] * xs) * _silu(zz)
    for g in range(SSM_GROUPS):
        ws = slice(g * SSM_GROUP_W, (g + 1) * SSM_GROUP_W)
        yg = y[:, ws]
        y_ref[:, ws] = yg * lax.rsqrt(jnp.mean(yg * yg, axis=-1, keepdims=True) + NORM_EPS)
    yn = (y_ref[...] * gn_ref[...]).astype(BF16)
    o_ref[...] = x_ref[...] + _dot(yn, wout_ref[...])


def _head_expand_matrix():
    e = np.zeros((LANES, SSM_D_INNER), np.float32)
    for h in range(SSM_HEADS):
        e[h, h * SSM_HEAD_DIM:(h + 1) * SSM_HEAD_DIM] = 1.0
    return jnp.asarray(e, dtype=BF16)


def _ssd(z, xbc, dt, x, batch, conv_w, conv_b, dt_bias, a_log, d_skip, g_norm, w_out):
    m, d = x.shape
    L = SSM_CHUNK
    nc = m // batch // L
    padh = lambda v: jnp.pad(v, (0, LANES - SSM_HEADS)).reshape(1, LANES)
    row = lambda n: pl.BlockSpec((L, n), lambda b, c: (b * nc + c, 0))
    return pl.pallas_call(
        _ssd_kernel,
        grid=(batch, nc),
        in_specs=[row(SSM_D_INNER), row(SSM_CONV_DIM), row(LANES), row(d),
                  _full((SSM_CONV, SSM_CONV_DIM)), _full((1, SSM_CONV_DIM)), _full((1, LANES)),
                  _full((1, LANES)), _full((1, SSM_D_INNER)), _full((1, SSM_D_INNER)),
                  _full((SSM_D_INNER, d)), _full((LANES, SSM_D_INNER))],
        out_specs=row(d),
        out_shape=jax.ShapeDtypeStruct((m, d), F32),
        scratch_shapes=[pltpu.VMEM((SSM_GROUPS, SSM_STATE, SSM_GROUP_W), F32),
                        pltpu.VMEM((L + 2 * SUBLANES, SSM_CONV_DIM), F32),
                        pltpu.VMEM((L, SSM_D_INNER), F32)],
        compiler_params=_params("parallel", "arbitrary"),
        name="ssm_scan",
    )(z, xbc, dt, x, conv_w.T, conv_b.reshape(1, -1), padh(dt_bias), padh(a_log),
      jnp.repeat(d_skip, SSM_HEAD_DIM).reshape(1, -1), g_norm.reshape(1, -1), w_out.astype(BF16),
      _head_expand_matrix())


def _router_kernel(r_ref, g_ref, rwh_ref, rwl_ref, rb_ref, xn_ref, meta_ref, cnt_ref, carry_ref):
    tm = r_ref.shape[0]
    i = pl.program_id(0)

    @pl.when(i == 0)
    def _():
        carry_ref[...] = jnp.zeros_like(carry_ref)

    xn = _rms_hat(r_ref[...]) * g_ref[...]
    xn_ref[...] = xn
    hb, hl = _split2(xn)
    logits = _dot(hb, rwh_ref[...]) + _dot(hl, rwh_ref[...]) + _dot(hb, rwl_ref[...]) + rb_ref[...]
    lane = lax.broadcasted_iota(jnp.int32, (tm, LANES), 1)
    vals, idxs = [], []
    cur = logits
    for _ in range(TOP_K):
        mx = jnp.max(cur, axis=-1, keepdims=True)
        ix = jnp.min(jnp.where(cur == mx, lane, LANES), axis=-1, keepdims=True)
        vals.append(mx)
        idxs.append(ix)
        cur = jnp.where(lane == ix, -jnp.inf, cur)
    es = [jnp.exp(v - vals[0]) for v in vals]
    den = es[0] + es[1] + es[2] + es[3]
    sel = jnp.zeros((tm, LANES), F32)
    for ix in idxs:
        sel = sel + jnp.where(lane == ix, 1.0, 0.0)
    row = lax.broadcasted_iota(jnp.int32, (tm, tm), 0)
    col = lax.broadcasted_iota(jnp.int32, (tm, tm), 1)
    strict = jnp.where(row > col, 1.0, 0.0).astype(BF16)
    before = _dot(strict, sel.astype(BF16)) + carry_ref[0:1, :]
    meta = jnp.zeros((tm, LANES), F32)
    for k in range(TOP_K):
        rank = jnp.sum(jnp.where(lane == idxs[k], before, 0.0), axis=-1, keepdims=True)
        meta = jnp.where(lane == k, idxs[k].astype(F32), meta)
        meta = jnp.where(lane == TOP_K + k, es[k] / den, meta)
        meta = jnp.where(lane == 2 * TOP_K + k, rank, meta)
    meta_ref[...] = meta
    carry = carry_ref[...] + jnp.sum(sel, axis=0, keepdims=True)
    carry_ref[...] = carry
    cnt_ref[...] = carry


def _router(r, g, r_w, r_b):
    m, d = r.shape
    tm = 512
    rw = jnp.pad(r_w, ((0, 0), (0, LANES - N_EXPERTS)))
    rwh, rwl = _split2(rw)
    rb = jnp.pad(r_b, (0, LANES - N_EXPERTS), constant_values=NEG).reshape(1, LANES)
    row = lambda n: pl.BlockSpec((tm, n), lambda i: (i, 0))
    return pl.pallas_call(
        _router_kernel,
        grid=(m // tm,),
        in_specs=[row(d), _full((1, d)), _full((d, LANES)), _full((d, LANES)), _full((1, LANES))],
        out_specs=[row(d), row(LANES), _full((SUBLANES, LANES))],
        out_shape=[jax.ShapeDtypeStruct((m, d), F32), jax.ShapeDtypeStruct((m, LANES), F32),
                   jax.ShapeDtypeStruct((SUBLANES, LANES), F32)],
        scratch_shapes=[pltpu.VMEM((SUBLANES, LANES), F32)],
        compiler_params=_params("arbitrary"),
        name="moe_router",
    )(r, g.reshape(1, d), rwh, rwl, rb)


def _dispatch_kernel(dest_ref, xn_ref, xs_in_ref, xs_ref, sem):
    del xs_in_ref
    tm = xn_ref.shape[0]

    def row_copy(t, k):
        d = dest_ref[t * TOP_K + k]
        return pltpu.make_async_copy(xn_ref.at[pl.ds(t, 1), :], xs_ref.at[pl.ds(d, 1), :], sem)

    def start(t, carry):
        for k in range(TOP_K):
            row_copy(t, k).start()
        return carry

    def wait(t, carry):
        for k in range(TOP_K):
            row_copy(t, k).wait()
        return carry

    lax.fori_loop(0, tm, start, 0)
    lax.fori_loop(0, tm, wait, 0)


def _dispatch(xn, dest_flat, n_rows):
    m, d = xn.shape
    tm = 256
    xs0 = jnp.zeros((n_rows, d), F32)
    return pl.pallas_call(
        _dispatch_kernel,
        grid=(m // tm,),
        in_specs=[pl.BlockSpec((tm * TOP_K,), lambda i: (i,), memory_space=pltpu.SMEM),
                  pl.BlockSpec((tm, d), lambda i: (i, 0)),
                  pl.BlockSpec(memory_space=pl.ANY)],
        out_specs=pl.BlockSpec(memory_space=pl.ANY),
        out_shape=jax.ShapeDtypeStruct((n_rows, d), F32),
        scratch_shapes=[pltpu.SemaphoreType.DMA(())],
        input_output_aliases={2: 0},
        compiler_params=_params("arbitrary"),
        name="moe_dispatch",
    )(dest_flat, xn, xs0)


def _expert_kernel(be_ref, nv_ref, xs_ref, wgu_ref, bgu_ref, wdn_ref, bdn_ref, perm_ref, ys_ref,
                   wgu_s, wdn_s):
    b = pl.program_id(0)
    e = be_ref[b]
    prev = be_ref[jnp.maximum(b - 1, 0)]
    d_ff2 = wgu_s.shape[1]

    @pl.when(jnp.logical_or(b == 0, e != prev))
    def _():
        for c in range(d_ff2 // MXU_DIM):
            cs = slice(c * MXU_DIM, (c + 1) * MXU_DIM)
            wgu_s[:, cs] = _dot(wgu_ref[0, :, cs].astype(BF16), perm_ref[...]).astype(BF16)
        wdn_s[...] = wdn_ref[0].astype(BF16)

    @pl.when(nv_ref[b] > 0)
    def _():
        h = _dot(xs_ref[...].astype(BF16), wgu_s[...]) + bgu_ref[0]
        acts = []
        for c in range(d_ff2 // MXU_DIM):
            gate = jnp.minimum(h[:, c * MXU_DIM:c * MXU_DIM + LANES], SWIGLU_LIMIT)
            up = jnp.clip(h[:, c * MXU_DIM + LANES:(c + 1) * MXU_DIM], -SWIGLU_LIMIT, SWIGLU_LIMIT)
            acts.append((up + 1.0) * (gate * jax.nn.sigmoid(SWIGLU_ALPHA * gate)))
        a = jnp.concatenate(acts, axis=1).astype(BF16)
        ys_ref[...] = _dot(a, wdn_s[...]) + bdn_ref[0]

    @pl.when(nv_ref[b] == 0)
    def _():
        ys_ref[...] = jnp.zeros_like(ys_ref)


def _deinterleave_matrix():
    p = np.zeros((MXU_DIM, MXU_DIM), np.float32)
    for j in range(LANES):
        p[2 * j, j] = 1.0
        p[2 * j + 1, LANES + j] = 1.0
    return jnp.asarray(p, dtype=BF16)


def _experts(xs, blk_e, blk_nv, w_gu, b_gu, w_dn, b_dn):
    n_rows, d = xs.shape
    n_e, _, d_ff2 = w_gu.shape
    d_ff = d_ff2 // 2
    tm = EXPERT_ROWS
    nt = d_ff2 // MXU_DIM
    bgu_p = b_gu.reshape(n_e, nt, LANES, 2).transpose(0, 1, 3, 2).reshape(n_e, 1, d_ff2)
    grid_spec = pltpu.PrefetchScalarGridSpec(
        num_scalar_prefetch=2,
        grid=(n_rows // tm,),
        in_specs=[pl.BlockSpec((tm, d), lambda b, be, nv: (b, 0)),
                  pl.BlockSpec((1, d, d_ff2), lambda b, be, nv: (be[b], 0, 0)),
                  pl.BlockSpec((1, 1, d_ff2), lambda b, be, nv: (be[b], 0, 0)),
                  pl.BlockSpec((1, d_ff, d), lambda b, be, nv: (be[b], 0, 0)),
                  pl.BlockSpec((1, 1, d), lambda b, be, nv: (be[b], 0, 0)),
                  pl.BlockSpec((MXU_DIM, MXU_DIM), lambda b, be, nv: (0, 0))],
        out_specs=pl.BlockSpec((tm, d), lambda b, be, nv: (b, 0)),
        scratch_shapes=[pltpu.VMEM((d, d_ff2), BF16), pltpu.VMEM((d_ff, d), BF16)],
    )
    return pl.pallas_call(
        _expert_kernel,
        grid_spec=grid_spec,
        out_shape=jax.ShapeDtypeStruct((n_rows, d), F32),
        compiler_params=_params("arbitrary"),
        name="moe_experts",
    )(blk_e, blk_nv, xs, w_gu, bgu_p, w_dn, b_dn.reshape(n_e, 1, d), _deinterleave_matrix())


def _combine_kernel(dest_ref, r_ref, meta_ref, p_ref, wple_ref, gple_ref, wgate_ref, ys_ref, o_ref,
                    buf, sem):
    tm = r_ref.shape[0]

    def row_copy(t, k):
        d = dest_ref[t * TOP_K + k]
        return pltpu.make_async_copy(ys_ref.at[pl.ds(d, 1), :], buf.at[k, pl.ds(t, 1), :], sem)

    def start(t, carry):
        for k in range(TOP_K):
            row_copy(t, k).start()
        return carry

    def wait(t, carry):
        for k in range(TOP_K):
            row_copy(t, k).wait()
        return carry

    lax.fori_loop(0, tm, start, 0)
    lax.fori_loop(0, tm, wait, 0)

    acc = r_ref[...]
    meta = meta_ref[...]
    for k in range(TOP_K):
        acc = acc + meta[:, TOP_K + k:TOP_K + k + 1] * buf[k]
    gate = jax.nn.sigmoid(_dot((_rms_hat(acc) * gple_ref[...]).astype(BF16), wgate_ref[...]))
    o_ref[...] = acc + _dot(p_ref[...].astype(BF16), wple_ref[...]) * gate


def _combine(r, meta, dest_flat, ys, p, w_ple, g_ple, w_gate):
    m, d = r.shape
    pd = p.shape[1]
    tm = 256
    row = lambda n: pl.BlockSpec((tm, n), lambda i: (i, 0))
    return pl.pallas_call(
        _combine_kernel,
        grid=(m // tm,),
        in_specs=[pl.BlockSpec((tm * TOP_K,), lambda i: (i,), memory_space=pltpu.SMEM),
                  row(d), row(LANES), row(pd), _full((pd, d)), _full((1, d)), _full((d, d)),
                  pl.BlockSpec(memory_space=pl.ANY)],
        out_specs=row(d),
        out_shape=jax.ShapeDtypeStruct((m, d), F32),
        scratch_shapes=[pltpu.VMEM((TOP_K, tm, d), F32), pltpu.SemaphoreType.DMA(())],
        compiler_params=_params("arbitrary"),
        name="moe_combine_ple",
    )(dest_flat, r, meta, p, w_ple.astype(BF16), g_ple.reshape(1, d), w_gate.astype(BF16), ys)


def _moe_ple(r, g_ffn, r_w, r_b, w_gu, b_gu, w_dn, b_dn, p, w_ple, g_ple, w_gate):
    m, d = r.shape
    xn, meta, cnt = _router(r, g_ffn, r_w, r_b)
    top_i = meta[:, 0:TOP_K].astype(jnp.int32)
    rank = meta[:, 2 * TOP_K:3 * TOP_K].astype(jnp.int32)
    counts = cnt[0, :N_EXPERTS].astype(jnp.int32)
    padded = ((counts + EXPERT_ROWS - 1) // EXPERT_ROWS) * EXPERT_ROWS
    pad_end = jnp.cumsum(padded)
    pad_start = pad_end - padded
    dest = (pad_start[top_i] + rank).reshape(-1)
    n_rows = m * TOP_K + N_EXPERTS * EXPERT_ROWS
    n_blk = n_rows // EXPERT_ROWS
    blk_start = jnp.arange(n_blk, dtype=jnp.int32) * EXPERT_ROWS
    blk_e = jnp.minimum(jnp.searchsorted(pad_end, blk_start, side='right'), N_EXPERTS - 1).astype(jnp.int32)
    blk_nv = jnp.clip(counts[blk_e] - (blk_start - pad_start[blk_e]), 0, EXPERT_ROWS).astype(jnp.int32)
    xs = _dispatch(xn, dest, n_rows)
    ys = _experts(xs, blk_e, blk_nv, w_gu, b_gu, w_dn, b_dn)
    return _combine(r, meta, dest, ys, p, w_ple, g_ple, w_gate)


def _seg_norm(x, bd, gain):
    w = x.shape[1]
    hi, lo = _split2(x * x)
    parts = []
    for c in range(w // MXU_DIM):
        cs = slice(c * MXU_DIM, (c + 1) * MXU_DIM)
        parts.append(_dot(hi[:, cs], bd) + _dot(lo[:, cs], bd))
    ss = parts[0] if len(parts) == 1 else jnp.concatenate(parts, axis=1)
    return x * lax.rsqrt(ss * (1.0 / ATT_HEAD_DIM) + NORM_EPS) * gain


def _rope64(x, cos, sin_signed):
    w = x.shape[1]
    half = ATT_HEAD_DIM // 2
    reps = w // LANES
    if reps > 1:
        cos = jnp.concatenate([cos] * reps, axis=1)
        sin_signed = jnp.concatenate([sin_signed] * reps, axis=1)
    lane = lax.broadcasted_iota(jnp.int32, x.shape, 1)
    first = (lane % ATT_HEAD_DIM) < half
    partner = jnp.where(first, pltpu.roll(x, w - half, axis=1), pltpu.roll(x, half, axis=1))
    return x * cos + partner * sin_signed


def _qkv_kernel(r_ref, gq_ref, gkv_ref, qw_ref, gw_ref, kvw_ref, qn_ref, kn_ref, cos_ref, sin_ref, bd_ref,
                q_ref, gates_ref, kcv_ref, kst_ref, kwt_ref, vs_ref, vw_ref):
    xhat = _rms_hat(r_ref[...])
    hq = (xhat * gq_ref[...]).astype(BF16)
    hkv = (xhat * gkv_ref[...]).astype(BF16)
    cos = cos_ref[...]
    sin = sin_ref[...]
    bd = bd_ref[...]
    gw = ATT_KV_GROUPS * ATT_HEAD_DIM

    q = _dot(hq, qw_ref[...])
    q = _rope64(_seg_norm(q, bd, qn_ref[...]), cos, sin) * (ATT_HEAD_DIM ** -0.5)
    for h in range(ATT_HEADS):
        q_ref[0, h] = q[:, h * ATT_HEAD_DIM:(h + 1) * ATT_HEAD_DIM].astype(BF16)
    gates_ref[...] = jax.nn.sigmoid(_dot(hq, gw_ref[...]))

    kv = _dot(hkv, kvw_ref[...])
    kcv_ref[...] = kv[:, 0:2 * gw]
    ks = _rope64(_seg_norm(kv[:, 2 * gw:3 * gw], bd, kn_ref[0:1, :]), cos, sin)
    kw = _rope64(_seg_norm(kv[:, 4 * gw:5 * gw], bd, kn_ref[1:2, :]), cos, sin)
    kst_ref[0] = ks.T.astype(BF16)
    kwt_ref[0] = kw.T.astype(BF16)
    vs_ref[...] = kv[:, 3 * gw:4 * gw].astype(BF16)
    vw_ref[...] = kv[:, 5 * gw:6 * gw].astype(BF16)


def _rope_tables(pos):
    dh = ATT_HEAD_DIM
    inv = 1.0 / (ROPE_THETA ** (jnp.arange(0, dh, 2, dtype=F32) / dh))
    ang = pos.astype(F32)[:, None] * inv
    c, s = jnp.cos(ang), jnp.sin(ang)
    cos = jnp.concatenate([c, c, c, c], axis=1)
    sin = jnp.concatenate([-s, s, -s, s], axis=1)
    return cos, sin


def _block_diag_ones():
    b = np.kron(np.eye(MXU_DIM // ATT_HEAD_DIM, dtype=np.float32),
                np.ones((ATT_HEAD_DIM, ATT_HEAD_DIM), np.float32))
    return jnp.asarray(b, dtype=BF16)


def _qkv(r, batch, g_q, g_kv, q_w, kv_w, q_norm, k_norm):
    m, d = r.shape
    seq = m // batch
    tm = 256
    nsb = seq // tm
    hd = ATT_HEADS * ATT_HEAD_DIM
    gw = ATT_KV_GROUPS * ATT_HEAD_DIM
    cos, sin = _rope_tables(jnp.arange(seq))
    qw = q_w[:, :hd].astype(BF16)
    gwt = jnp.pad(q_w[:, hd:], ((0, 0), (0, LANES - N_BRANCH * ATT_HEADS))).astype(BF16)
    qn = jnp.tile(q_norm, ATT_HEADS).reshape(1, hd)
    kn = jnp.stack([jnp.tile(k_norm[1], ATT_KV_GROUPS), jnp.tile(k_norm[2], ATT_KV_GROUPS)])
    row = lambda n: pl.BlockSpec((tm, n), lambda i: (i, 0))
    tab = pl.BlockSpec((tm, LANES), lambda i: (i % nsb, 0))
    kt = pl.BlockSpec((1, gw, tm), lambda i: (i // nsb, 0, i % nsb))
    return pl.pallas_call(
        _qkv_kernel,
        grid=(m // tm,),
        in_specs=[row(d), _full((1, d)), _full((1, d)), _full((d, hd)), _full((d, LANES)),
                  _full((d, 6 * gw)), _full((1, hd)), _full((2, gw)), tab, tab,
                  _full((MXU_DIM, MXU_DIM))],
        out_specs=[pl.BlockSpec((1, ATT_HEADS, tm, ATT_HEAD_DIM), lambda i: (i // nsb, 0, i % nsb, 0)),
                   row(LANES), row(2 * gw), kt, kt, row(gw), row(gw)],
        out_shape=[jax.ShapeDtypeStruct((batch, ATT_HEADS, seq, ATT_HEAD_DIM), BF16),
                   jax.ShapeDtypeStruct((m, LANES), F32),
                   jax.ShapeDtypeStruct((m, 2 * gw), F32),
                   jax.ShapeDtypeStruct((batch, gw, seq), BF16),
                   jax.ShapeDtypeStruct((batch, gw, seq), BF16),
                   jax.ShapeDtypeStruct((m, gw), BF16),
                   jax.ShapeDtypeStruct((m, gw), BF16)],
        compiler_params=_params("parallel"),
        name="nsa_qkv",
    )(r, g_q.reshape(1, d), g_kv.reshape(1, d), qw, gwt, kv_w.astype(BF16), qn, kn, cos, sin,
      _block_diag_ones())


def _cmp_kernel(x_ref, pe_ref, w1_ref, w2_ref, kn_ref, cos_ref, sin_ref, o_ref, ot_ref):
    nh = x_ref.shape[1]
    half_w = x_ref.shape[2]
    x = x_ref[0]
    h1 = _dot((x + pe_ref[0, 0:1, :]).astype(BF16), w1_ref[0, 0:half_w, :])
    h2 = _dot((x + pe_ref[0, 1:2, :]).astype(BF16), w1_ref[0, half_w:2 * half_w, :])
    pre = h1 + pltpu.roll(h2, nh - 1, axis=0)
    c = _dot(_silu(pre).astype(BF16), w2_ref[0])
    is_k = (pl.program_id(0) // ATT_KV_GROUPS) % 2 == 0
    cn = c * lax.rsqrt(jnp.sum(c * c, axis=-1, keepdims=True) * (1.0 / ATT_HEAD_DIM) + NORM_EPS) * kn_ref[...]
    cn = _rope64(cn, cos_ref[...], sin_ref[...])
    res = jnp.where(is_k, cn, c)
    rowi = lax.broadcasted_iota(jnp.int32, res.shape, 0)
    res = jnp.where(rowi < nh - 1, res, 0.0)
    o_ref[0] = res[:, 0:ATT_HEAD_DIM].astype(BF16)
    ot_ref[0] = res.T[0:ATT_HEAD_DIM, :].astype(BF16)


def _compress(kcv, batch, cmp_pe, cmp_w1, cmp_w2, k_norm0):
    m = kcv.shape[0]
    seq = m // batch
    nh = seq // CMP_STRIDE
    dh = ATT_HEAD_DIM
    half_w = CMP_STRIDE * dh
    hidden = cmp_w1.shape[2]
    x = kcv.reshape(batch, seq, 2, ATT_KV_GROUPS, dh).transpose(0, 2, 3, 1, 4)
    x = x.reshape(batch * 2 * ATT_KV_GROUPS, nh, half_w)
    pe = cmp_pe.reshape(2, 2, half_w)
    w2 = jnp.pad(cmp_w2, ((0, 0), (0, 0), (0, LANES - dh))).astype(BF16)
    kn = jnp.pad(k_norm0, (0, LANES - dh)).reshape(1, LANES)
    cos, sin = _rope_tables(jnp.arange(nh) * CMP_STRIDE + (CMP_BLOCK - 1))
    nidx = lambda i: (i // ATT_KV_GROUPS) % 2
    n_all = batch * 2 * ATT_KV_GROUPS
    c, ct = pl.pallas_call(
        _cmp_kernel,
        grid=(n_all,),
        in_specs=[pl.BlockSpec((1, nh, half_w), lambda i: (i, 0, 0)),
                  pl.BlockSpec((1, 2, half_w), lambda i: (nidx(i), 0, 0)),
                  pl.BlockSpec((1, 2 * half_w, hidden), lambda i: (nidx(i), 0, 0)),
                  pl.BlockSpec((1, hidden, LANES), lambda i: (nidx(i), 0, 0)),
                  _full((1, LANES)), _full((nh, LANES)), _full((nh, LANES))],
        out_specs=[pl.BlockSpec((1, nh, dh), lambda i: (i, 0, 0)),
                   pl.BlockSpec((1, dh, nh), lambda i: (i, 0, 0))],
        out_shape=[jax.ShapeDtypeStruct((n_all, nh, dh), BF16),
                   jax.ShapeDtypeStruct((n_all, dh, nh), BF16)],
        compiler_params=_params("parallel"),
        name="nsa_compress",
    )(x, pe, cmp_w1.astype(BF16), w2, kn, cos, sin)
    return (c.reshape(batch, 2, ATT_KV_GROUPS, nh, dh), ct.reshape(batch, 2, ATT_KV_GROUPS, dh, nh))


def _attn_kernel(q_ref, gates_ref, kct_ref, vc_ref, kst_ref, vs_ref, kwt_ref, vw_ref, selw_ref, exp_ref,
                 o_ref):
    i = pl.program_id(2)
    rows = ATT_REP * Q_BLOCK
    n_cmp = kct_ref.shape[-1]
    n_sel = selw_ref.shape[1]
    t0 = i * Q_BLOCK
    q = q_ref[0].reshape(rows, ATT_HEAD_DIM)
    t_rows = t0 + lax.broadcasted_iota(jnp.int32, (rows, 1), 0) % Q_BLOCK
    t_q = t0 + lax.broadcasted_iota(jnp.int32, (Q_BLOCK, 1), 0)

    s = _dot(q, kct_ref[0, 0, 0])
    cend = lax.broadcasted_iota(jnp.int32, (rows, n_cmp), 1) * CMP_STRIDE + (CMP_BLOCK - 1)
    ok = cend <= t_rows
    sm = jnp.where(ok, s, NEG)
    p = jnp.where(ok, jnp.exp(sm - jnp.max(sm, axis=-1, keepdims=True)), 0.0)
    den = jnp.sum(p, axis=-1, keepdims=True)
    pc = p / jnp.where(den > 0, den, 1.0)
    o_c = _dot(pc.astype(BF16), vc_ref[0, 0, 0])

    psum = pc[0:Q_BLOCK]
    for r in range(1, ATT_REP):
        psum = psum + pc[r * Q_BLOCK:(r + 1) * Q_BLOCK]
    imp = _dot_parts(_split3(psum), selw_ref[...])
    j = lax.broadcasted_iota(jnp.int32, (Q_BLOCK, n_sel), 1)
    cur = t_q // SEL_BLOCK
    valid = j <= cur
    forced = (j == 0) | (j == cur) | (j == cur - 1)
    sc = jnp.where(valid, imp + jnp.where(forced, SEL_FORCE, 0.0), -1.0)
    sel = jnp.zeros((Q_BLOCK, n_sel), F32)
    for _ in range(min(SEL_TOPK, n_sel)):
        mx = jnp.max(sc, axis=-1, keepdims=True)
        ix = jnp.min(jnp.where(sc == mx, j, n_sel), axis=-1, keepdims=True)
        pick = j == ix
        sel = jnp.where(pick, 1.0, sel)
        sc = jnp.where(pick, -jnp.inf, sc)
    selb = jnp.where(valid, sel, 0.0).astype(BF16)

    def chunk(c, carry, diag):
        m_run, l_run, acc = carry
        off = pl.multiple_of(c * SEL_CHUNK, SEL_CHUNK)
        sck = _dot(q, kst_ref[0, 0, :, pl.ds(off, SEL_CHUNK)])
        mk = _dot(selb, exp_ref[:, pl.ds(off, SEL_CHUNK)])
        if diag:
            kpos = off + lax.broadcasted_iota(jnp.int32, (Q_BLOCK, SEL_CHUNK), 1)
            mk = jnp.where(kpos <= t_q, mk, 0.0)
        mk4 = jnp.concatenate([mk] * ATT_REP, axis=0)
        smk = jnp.where(mk4 > 0.5, sck, NEG)
        m_new = jnp.maximum(m_run, jnp.max(smk, axis=-1, keepdims=True))
        alpha = jnp.exp(m_run - m_new)
        pk = jnp.exp(smk - m_new)
        l_new = alpha * l_run + jnp.sum(pk, axis=-1, keepdims=True)
        acc_new = alpha * acc + _dot(pk.astype(BF16), vs_ref[0, 0, pl.ds(off, SEL_CHUNK), :])
        return m_new, l_new, acc_new

    n_full = t0 // SEL_CHUNK
    init = (jnp.full((rows, 1), NEG, F32), jnp.zeros((rows, 1), F32), jnp.zeros((rows, ATT_HEAD_DIM), F32))
    carry = lax.fori_loop(0, n_full, lambda c, cr: chunk(c, cr, False), init)
    _, l_s, acc_s = chunk(n_full, carry, True)
    o_s = acc_s / l_s

    wk = WINDOW + Q_BLOCK
    st = pl.multiple_of(jnp.maximum(t0 - WINDOW, 0), LANES)
    sw = _dot(q, kwt_ref[0, 0, :, pl.ds(st, wk)])
    kpos = st + lax.broadcasted_iota(jnp.int32, (rows, wk), 1)
    okw = (kpos <= t_rows) & (kpos > t_rows - WINDOW)
    smw = jnp.where(okw, sw, NEG)
    pw = jnp.exp(smw - jnp.max(smw, axis=-1, keepdims=True))
    o_w = _dot(pw.astype(BF16), vw_ref[0, 0, pl.ds(st, wk), :]) / jnp.sum(pw, axis=-1, keepdims=True)

    gts = gates_ref[0, 0, 0]
    o = gts[:, 0:1] * o_c + gts[:, 1:2] * o_s + gts[:, 2:3] * o_w
    for r in range(ATT_REP):
        o_ref[0, :, r * ATT_HEAD_DIM:(r + 1) * ATT_HEAD_DIM] = o[r * Q_BLOCK:(r + 1) * Q_BLOCK]


def _cmp_to_sel_weights(seq):
    n_cmp = (seq - CMP_BLOCK) // CMP_STRIDE + 1
    n_sel = seq // SEL_BLOCK
    cs = np.arange(n_cmp) * CMP_STRIDE
    ss = np.arange(n_sel) * SEL_BLOCK
    ov = np.minimum(cs[:, None] + CMP_BLOCK, ss[None, :] + SEL_BLOCK) - np.maximum(cs[:, None], ss[None, :])
    w = np.clip(ov, 0, None).astype(np.float32) / CMP_BLOCK
    w = np.concatenate([w, np.zeros((seq // CMP_STRIDE - n_cmp, n_sel), np.float32)], axis=0)
    return jnp.asarray(w, dtype=BF16)


def _sel_expand_matrix(seq):
    n_sel = seq // SEL_BLOCK
    e = (np.arange(n_sel)[:, None] == (np.arange(seq)[None, :] // SEL_BLOCK)).astype(np.float32)
    return jnp.asarray(e, dtype=BF16)


def _attention(q, gates, kc, kct, kst, kwt, vs, vw, batch):
    seq = q.shape[2]
    nqb = seq // Q_BLOCK
    dh = ATT_HEAD_DIM
    n_cmp = seq // CMP_STRIDE
    n_sel = seq // SEL_BLOCK
    rows = ATT_REP * Q_BLOCK
    hd = ATT_HEADS * dh
    gw = ATT_KV_GROUPS * dh
    g3 = gates[:, :N_BRANCH * ATT_HEADS].reshape(batch, nqb, Q_BLOCK, ATT_KV_GROUPS, ATT_REP, N_BRANCH)
    g3 = g3.transpose(0, 3, 1, 4, 2, 5).reshape(batch, ATT_KV_GROUPS, nqb, rows, N_BRANCH)
    to_groups = lambda v: v.reshape(batch, seq, ATT_KV_GROUPS, dh).transpose(0, 2, 1, 3)
    kst4 = kst.reshape(batch, ATT_KV_GROUPS, dh, seq)
    kwt4 = kwt.reshape(batch, ATT_KV_GROUPS, dh, seq)
    per_bg = lambda *shape: pl.BlockSpec((1, 1) + shape, lambda b, g, i: (b, g, 0, 0))
    return pl.pallas_call(
        _attn_kernel,
        grid=(batch, ATT_KV_GROUPS, nqb),
        in_specs=[pl.BlockSpec((1, ATT_REP, Q_BLOCK, dh), lambda b, g, i: (b, g, i, 0)),
                  pl.BlockSpec((1, 1, 1, rows, N_BRANCH), lambda b, g, i: (b, g, i, 0, 0)),
                  pl.BlockSpec((1, 1, 1, dh, n_cmp), lambda b, g, i: (b, 0, g, 0, 0)),
                  pl.BlockSpec((1, 1, 1, n_cmp, dh), lambda b, g, i: (b, 1, g, 0, 0)),
                  per_bg(dh, seq), per_bg(seq, dh), per_bg(dh, seq), per_bg(seq, dh),
                  pl.BlockSpec((n_cmp, n_sel), lambda b, g, i: (0, 0)),
                  pl.BlockSpec((n_sel, seq), lambda b, g, i: (0, 0))],
        out_specs=pl.BlockSpec((1, Q_BLOCK, gw), lambda b, g, i: (b, i, g)),
        out_shape=jax.ShapeDtypeStruct((batch, seq, hd), F32),
        compiler_params=_params("parallel", "parallel", "arbitrary"),
        name="nsa_attention",
    )(q, g3, kct, kc, kst4, to_groups(vs), kwt4, to_groups(vw), _cmp_to_sel_weights(seq),
      _sel_expand_matrix(seq))


def _oproj_kernel(a_ref, r_ref, w_ref, o_ref):
    o_ref[...] = r_ref[...] + _dot(a_ref[...].astype(BF16), w_ref[...])


def _oproj(a, r, w):
    m, d = r.shape
    k = a.shape[1]
    tm = 512
    row = lambda n: pl.BlockSpec((tm, n), lambda i: (i, 0))
    return pl.pallas_call(
        _oproj_kernel,
        grid=(m // tm,),
        in_specs=[row(k), row(d), _full((k, d))],
        out_specs=row(d),
        out_shape=jax.ShapeDtypeStruct((m, d), F32),
        compiler_params=_params("parallel"),
        name="nsa_oproj",
    )(a, r, w.astype(BF16))


def kernel(x, p, g_mix, g_ffn, m_w_in, m_conv_w, m_conv_b, m_dt_bias, m_a_log, m_d, m_g_norm, m_w_out,
           kv_g, kv_w, cmp_pe, cmp_w1, cmp_w2, k_norm, q_w, q_norm, o_w,
           r_w, r_b, e_w_gu, e_b_gu, e_w_dn, e_b_dn, ple_w, ple_g, ple_gate_w):
    batch, seq, d = x.shape
    m = batch * seq
    depth = p.shape[0]
    n_a = m_w_in.shape[0]
    r = x.reshape(m, d)
    pf = p.reshape(depth, m, p.shape[-1])
    shared = None
    for i in range(depth):
        if i < n_a:
            z, xbc, dt = _inproj(r, g_mix[i], m_w_in[i])
            r = _ssd(z, xbc, dt, r, batch, m_conv_w[i], m_conv_b[i], m_dt_bias[i], m_a_log[i], m_d[i],
                     m_g_norm[i], m_w_out[i])
        else:
            jb = i - n_a
            q, gates, kcv, kst, kwt, vs, vw = _qkv(r, batch, g_mix[i], kv_g, q_w[jb], kv_w, q_norm[jb], k_norm)
            if shared is None:
                shared = _compress(kcv, batch, cmp_pe, cmp_w1, cmp_w2, k_norm[0]) + (kst, kwt, vs, vw)
            a = _attention(q, gates, *shared, batch)
            r = _oproj(a.reshape(m, -1), r, o_w[jb])
        r = _moe_ple(r, g_ffn[i], r_w[i], r_b[i], e_w_gu[i], e_b_gu[i], e_w_dn[i], e_b_dn[i],
                     pf[i], ple_w[i], ple_g[i], ple_gate_w[i])
    return r.reshape(batch, seq, d)
```

```python
import functools
import math

import numpy as np
import jax
import jax.numpy as jnp
from jax import lax
from jax.experimental import pallas as pl
from jax.experimental.pallas import tpu as pltpu

F32 = jnp.float32
BF16 = jnp.bfloat16

NORM_EPS = 1e-6
ROPE_THETA = 10000.0
SSM_HEADS = 32
SSM_HEAD_DIM = 64
SSM_GROUPS = 4
SSM_STATE = 128
SSM_CONV = 4
SSM_CHUNK = 256
SSM_D_INNER = SSM_HEADS * SSM_HEAD_DIM
SSM_GN = SSM_GROUPS * SSM_STATE
SSM_CONV_DIM = SSM_D_INNER + 2 * SSM_GN
SSM_GROUP_W = SSM_D_INNER // SSM_GROUPS
ATT_HEAD_DIM = 64
ATT_KV_GROUPS = 4
ATT_REP = 4
ATT_HEADS = ATT_KV_GROUPS * ATT_REP
N_BRANCH = 3
CMP_BLOCK = 32
CMP_STRIDE = 16
SEL_BLOCK = 64
SEL_TOPK = 16
WINDOW = 512
Q_BLOCK = 128
SEL_FORCE = 1000.0
N_FORCED = 3
SEL_CHUNK = 512
GATE_ROWS = 16
V_ROWS = 80
LOG2E = 1.4426950408889634
N_EXPERTS = 32
TOP_K = 4
SWIGLU_LIMIT = 7.0
SWIGLU_ALPHA = 1.702
EXPERT_ROWS = 256

LANES = 128
SUBLANES = 8
MXU_DIM = 256
VMEM_LIMIT_BYTES = 56 * 1024 * 1024
NEG = -1e30


def _params(*sem):
    return pltpu.CompilerParams(dimension_semantics=sem, vmem_limit_bytes=VMEM_LIMIT_BYTES)


def _dot(a, b):
    return jnp.dot(a, b, preferred_element_type=F32)


def _dot_nt(a, b):
    return lax.dot_general(a, b, (((1,), (1,)), ((), ())), preferred_element_type=F32)


def _split2(x):
    hi = x.astype(BF16)
    lo = (x - hi.astype(F32)).astype(BF16)
    return hi, lo


def _split3(x):
    hi = x.astype(BF16)
    r = x - hi.astype(F32)
    mid = r.astype(BF16)
    lo = (r - mid.astype(F32)).astype(BF16)
    return hi, mid, lo


def _dot_parts(parts, m):
    out = _dot(parts[0], m)
    for p in parts[1:]:
        out = out + _dot(p, m)
    return out


def _rms_hat(x):
    return x * lax.rsqrt(jnp.mean(x * x, axis=-1, keepdims=True) + NORM_EPS)


def _silu(x):
    return x * jax.nn.sigmoid(x)


def _full(shape):
    nd = len(shape)
    return pl.BlockSpec(shape, lambda *_: (0,) * nd)


def _inproj_kernel(x_ref, g_ref, wz_ref, wx_ref, wdh_ref, wdl_ref, z_ref, xbc_ref, dt_ref):
    h = _rms_hat(x_ref[...]) * g_ref[...]
    hb, hl = _split2(h)
    z_ref[...] = _dot(hb, wz_ref[...])
    xbc_ref[...] = _dot(hb, wx_ref[...])
    dt_ref[...] = _dot(hb, wdh_ref[...]) + _dot(hl, wdh_ref[...]) + _dot(hb, wdl_ref[...])


def _inproj(x, g, w_in):
    m, d = x.shape
    tm = 256
    wz = w_in[:, :SSM_D_INNER].astype(BF16)
    wx = w_in[:, SSM_D_INNER:SSM_D_INNER + SSM_CONV_DIM].astype(BF16)
    wd = jnp.pad(w_in[:, SSM_D_INNER + SSM_CONV_DIM:], ((0, 0), (0, LANES - SSM_HEADS)))
    wdh, wdl = _split2(wd)
    row = lambda n: pl.BlockSpec((tm, n), lambda i: (i, 0))
    return pl.pallas_call(
        _inproj_kernel,
        grid=(m // tm,),
        in_specs=[row(d), _full((1, d)), _full((d, SSM_D_INNER)), _full((d, SSM_CONV_DIM)),
                  _full((d, LANES)), _full((d, LANES))],
        out_specs=[row(SSM_D_INNER), row(SSM_CONV_DIM), row(LANES)],
        out_shape=[jax.ShapeDtypeStruct((m, SSM_D_INNER), F32),
                   jax.ShapeDtypeStruct((m, SSM_CONV_DIM), F32),
                   jax.ShapeDtypeStruct((m, LANES), F32)],
        compiler_params=_params("parallel"),
        name="ssm_inproj",
    )(x, g.reshape(1, d), wz, wx, wdh, wdl)


def _ssd_kernel(z_ref, xbc_ref, dt_ref, x_ref, cw_ref, cb_ref, dtb_ref, alog_ref, dsk_ref, gn_ref,
                wout_ref, exp_ref, o_ref, state_ref, ext_ref, y_ref):
    L = SSM_CHUNK
    c = pl.program_id(1)

    @pl.when(c == 0)
    def _():
        state_ref[...] = jnp.zeros_like(state_ref)
        ext_ref[0:SUBLANES, :] = jnp.zeros((SUBLANES, SSM_CONV_DIM), F32)

    ext_ref[SUBLANES:SUBLANES + L, :] = xbc_ref[...]
    acc = jnp.broadcast_to(cb_ref[...], (L, SSM_CONV_DIM))
    for k in range(SSM_CONV):
        off = SUBLANES - (SSM_CONV - 1) + k
        acc = acc + cw_ref[k:k + 1, :] * ext_ref[off:off + L, :]
    ext_ref[0:SUBLANES, :] = ext_ref[L:L + SUBLANES, :]
    xc = _silu(acc)
    xs = xc[:, :SSM_D_INNER]
    bm = xc[:, SSM_D_INNER:SSM_D_INNER + SSM_GN]
    cm = xc[:, SSM_D_INNER + SSM_GN:]

    dtr = dt_ref[...] + dtb_ref[...]
    dt = jnp.maximum(dtr, 0.0) + jnp.log(1.0 + jnp.exp(-jnp.abs(dtr)))
    a_neg = -jnp.exp(alog_ref[...])
    da = dt * a_neg
    row = lax.broadcasted_iota(jnp.int32, (L, L), 0)
    col = lax.broadcasted_iota(jnp.int32, (L, L), 1)
    causal = row >= col
    tri = jnp.where(causal, 1.0, 0.0).astype(BF16)
    da_hi, da_mid, da_lo = _split3(da)
    acs = _dot(tri, da_hi) + _dot(tri, da_mid) + _dot(tri, da_lo)
    a_last = acs[L - 1:L, :]
    d_end = jnp.exp(a_last - acs)
    d_start = jnp.exp(acs)
    cdec = jnp.broadcast_to(jnp.exp(a_last), (SUBLANES, LANES))

    ex = exp_ref[...]
    dt_e = _dot(dt.astype(BF16), ex)
    d_end_e = _dot(d_end.astype(BF16), ex)
    d_start_e = _dot_parts(_split2(d_start), ex)
    cdec_e = _dot_parts(_split2(cdec), ex)[0:1, :]

    xdt = xs * dt_e
    xb = xdt.astype(BF16)
    xd = (xdt * d_end_e).astype(BF16)
    acs_t = acs.T
    bm_t = bm.T
    bmb = bm.astype(BF16)
    cmb = cm.astype(BF16)

    for g in range(SSM_GROUPS):
        gs = slice(g * SSM_STATE, (g + 1) * SSM_STATE)
        ws = slice(g * SSM_GROUP_W, (g + 1) * SSM_GROUP_W)
        cb = _dot_nt(cmb[:, gs], bmb[:, gs])
        for r in range(SSM_HEADS // SSM_GROUPS):
            h = g * (SSM_HEADS // SSM_GROUPS) + r
            hs = slice(h * SSM_HEAD_DIM, (h + 1) * SSM_HEAD_DIM)
            diff = acs[:, h:h + 1] - acs_t[h:h + 1, :]
            lm = jnp.exp(jnp.where(causal, diff, -jnp.inf))
            y_ref[:, hs] = _dot((cb * lm).astype(BF16), xb[:, hs])
        st = state_ref[g]
        y_ref[:, ws] = y_ref[:, ws] + _dot(cmb[:, gs], st.astype(BF16)) * d_start_e[:, ws]
        state_ref[g] = cdec_e[:, ws] * st + _dot(bm_t[gs, :].astype(BF16), xd[:, ws])

    zz = z_ref[...]
    y = (y_ref[...] + dsk_ref[...] * xs) * _silu(zz)
    for g in range(SSM_GROUPS):
        ws = slice(g * SSM_GROUP_W, (g + 1) * SSM_GROUP_W)
        yg = y[:, ws]
        y_ref[:, ws] = yg * lax.rsqrt(jnp.mean(yg * yg, axis=-1, keepdims=True) + NORM_EPS)
    yn = (y_ref[...] * gn_ref[...]).astype(BF16)
    o_ref[...] = x_ref[...] + _dot(yn, wout_ref[...])


def _head_expand_matrix():
    e = np.zeros((LANES, SSM_D_INNER), np.float32)
    for h in range(SSM_HEADS):
        e[h, h * SSM_HEAD_DIM:(h + 1) * SSM_HEAD_DIM] = 1.0
    return jnp.asarray(e, dtype=BF16)


def _ssd(z, xbc, dt, x, batch, conv_w, conv_b, dt_bias, a_log, d_skip, g_norm, w_out):
    m, d = x.shape
    L = SSM_CHUNK
    nc = m // batch // L
    padh = lambda v: jnp.pad(v, (0, LANES - SSM_HEADS)).reshape(1, LANES)
    row = lambda n: pl.BlockSpec((L, n), lambda b, c: (b * nc + c, 0))
    return pl.pallas_call(
        _ssd_kernel,
        grid=(batch, nc),
        in_specs=[row(SSM_D_INNER), row(SSM_CONV_DIM), row(LANES), row(d),
                  _full((SSM_CONV, SSM_CONV_DIM)), _full((1, SSM_CONV_DIM)), _full((1, LANES)),
                  _full((1, LANES)), _full((1, SSM_D_INNER)), _full((1, SSM_D_INNER)),
                  _full((SSM_D_INNER, d)), _full((LANES, SSM_D_INNER))],
        out_specs=row(d),
        out_shape=jax.ShapeDtypeStruct((m, d), F32),
        scratch_shapes=[pltpu.VMEM((SSM_GROUPS, SSM_STATE, SSM_GROUP_W), F32),
                        pltpu.VMEM((L + 2 * SUBLANES, SSM_CONV_DIM), F32),
                        pltpu.VMEM((L, SSM_D_INNER), F32)],
        compiler_params=_params("parallel", "arbitrary"),
        name="ssm_scan",
    )(z, xbc, dt, x, conv_w.T, conv_b.reshape(1, -1), padh(dt_bias), padh(a_log),
      jnp.repeat(d_skip, SSM_HEAD_DIM).reshape(1, -1), g_norm.reshape(1, -1), w_out.astype(BF16),
      _head_expand_matrix())


def _router_kernel(r_ref, g_ref, rwh_ref, rwl_ref, rb_ref, xn_ref, meta_ref, cnt_ref, carry_ref):
    tm = r_ref.shape[0]
    i = pl.program_id(0)

    @pl.when(i == 0)
    def _():
        carry_ref[...] = jnp.zeros_like(carry_ref)

    xn = _rms_hat(r_ref[...]) * g_ref[...]
    xn_ref[...] = xn
    hb, hl = _split2(xn)
    logits = _dot(hb, rwh_ref[...]) + _dot(hl, rwh_ref[...]) + _dot(hb, rwl_ref[...]) + rb_ref[...]
    lane = lax.broadcasted_iota(jnp.int32, (tm, LANES), 1)
    vals, idxs = [], []
    cur = logits
    for _ in range(TOP_K):
        mx = jnp.max(cur, axis=-1, keepdims=True)
        ix = jnp.min(jnp.where(cur == mx, lane, LANES), axis=-1, keepdims=True)
        vals.append(mx)
        idxs.append(ix)
        cur = jnp.where(lane == ix, -jnp.inf, cur)
    es = [jnp.exp(v - vals[0]) for v in vals]
    den = es[0] + es[1] + es[2] + es[3]
    sel = jnp.zeros((tm, LANES), F32)
    for ix in idxs:
        sel = sel + jnp.where(lane == ix, 1.0, 0.0)
    row = lax.broadcasted_iota(jnp.int32, (tm, tm), 0)
    col = lax.broadcasted_iota(jnp.int32, (tm, tm), 1)
    strict = jnp.where(row > col, 1.0, 0.0).astype(BF16)
    before = _dot(strict, sel.astype(BF16)) + carry_ref[0:1, :]
    meta = jnp.zeros((tm, LANES), F32)
    for k in range(TOP_K):
        rank = jnp.sum(jnp.where(lane == idxs[k], before, 0.0), axis=-1, keepdims=True)
        meta = jnp.where(lane == k, idxs[k].astype(F32), meta)
        meta = jnp.where(lane == TOP_K + k, es[k] / den, meta)
        meta = jnp.where(lane == 2 * TOP_K + k, rank, meta)
    meta_ref[...] = meta
    carry = carry_ref[...] + jnp.sum(sel, axis=0, keepdims=True)
    carry_ref[...] = carry
    cnt_ref[...] = carry


def _router(r, g, r_w, r_b):
    m, d = r.shape
    tm = 512
    rw = jnp.pad(r_w, ((0, 0), (0, LANES - N_EXPERTS)))
    rwh, rwl = _split2(rw)
    rb = jnp.pad(r_b, (0, LANES - N_EXPERTS), constant_values=NEG).reshape(1, LANES)
    row = lambda n: pl.BlockSpec((tm, n), lambda i: (i, 0))
    return pl.pallas_call(
        _router_kernel,
        grid=(m // tm,),
        in_specs=[row(d), _full((1, d)), _full((d, LANES)), _full((d, LANES)), _full((1, LANES))],
        out_specs=[row(d), row(LANES), _full((SUBLANES, LANES))],
        out_shape=[jax.ShapeDtypeStruct((m, d), F32), jax.ShapeDtypeStruct((m, LANES), F32),
                   jax.ShapeDtypeStruct((SUBLANES, LANES), F32)],
        scratch_shapes=[pltpu.VMEM((SUBLANES, LANES), F32)],
        compiler_params=_params("arbitrary"),
        name="moe_router",
    )(r, g.reshape(1, d), rwh, rwl, rb)


def _dispatch_kernel(dest_ref, xn_ref, xs_in_ref, xs_ref, sem):
    del xs_in_ref
    tm = xn_ref.shape[0]

    def row_copy(t, k):
        d = dest_ref[t * TOP_K + k]
        return pltpu.make_async_copy(xn_ref.at[pl.ds(t, 1), :], xs_ref.at[pl.ds(d, 1), :], sem)

    def start(t, carry):
        for k in range(TOP_K):
            row_copy(t, k).start()
        return carry

    def wait(t, carry):
        for k in range(TOP_K):
            row_copy(t, k).wait()
        return carry

    lax.fori_loop(0, tm, start, 0)
    lax.fori_loop(0, tm, wait, 0)


def _dispatch(xn, dest_flat, n_rows):
    m, d = xn.shape
    tm = 256
    xs0 = jnp.zeros((n_rows, d), F32)
    return pl.pallas_call(
        _dispatch_kernel,
        grid=(m // tm,),
        in_specs=[pl.BlockSpec((tm * TOP_K,), lambda i: (i,), memory_space=pltpu.SMEM),
                  pl.BlockSpec((tm, d), lambda i: (i, 0)),
                  pl.BlockSpec(memory_space=pl.ANY)],
        out_specs=pl.BlockSpec(memory_space=pl.ANY),
        out_shape=jax.ShapeDtypeStruct((n_rows, d), F32),
        scratch_shapes=[pltpu.SemaphoreType.DMA(())],
        input_output_aliases={2: 0},
        compiler_params=_params("arbitrary"),
        name="moe_dispatch",
    )(dest_flat, xn, xs0)


def _expert_kernel(be_ref, nv_ref, xs_ref, wgu_ref, bgu_ref, wdn_ref, bdn_ref, perm_ref, ys_ref,
                   wgu_s, wdn_s):
    b = pl.program_id(0)
    e = be_ref[b]
    prev = be_ref[jnp.maximum(b - 1, 0)]
    d_ff2 = wgu_s.shape[1]

    @pl.when(jnp.logical_or(b == 0, e != prev))
    def _():
        for c in range(d_ff2 // MXU_DIM):
            cs = slice(c * MXU_DIM, (c + 1) * MXU_DIM)
            wgu_s[:, cs] = _dot(wgu_ref[0, :, cs].astype(BF16), perm_ref[...]).astype(BF16)
        wdn_s[...] = wdn_ref[0].astype(BF16)

    @pl.when(nv_ref[b] > 0)
    def _():
        h = _dot(xs_ref[...].astype(BF16), wgu_s[...]) + bgu_ref[0]
        acts = []
        for c in range(d_ff2 // MXU_DIM):
            gate = jnp.minimum(h[:, c * MXU_DIM:c * MXU_DIM + LANES], SWIGLU_LIMIT)
            up = jnp.clip(h[:, c * MXU_DIM + LANES:(c + 1) * MXU_DIM], -SWIGLU_LIMIT, SWIGLU_LIMIT)
            acts.append((up + 1.0) * (gate * jax.nn.sigmoid(SWIGLU_ALPHA * gate)))
        a = jnp.concatenate(acts, axis=1).astype(BF16)
        ys_ref[...] = _dot(a, wdn_s[...]) + bdn_ref[0]

    @pl.when(nv_ref[b] == 0)
    def _():
        ys_ref[...] = jnp.zeros_like(ys_ref)


def _deinterleave_matrix():
    p = np.zeros((MXU_DIM, MXU_DIM), np.float32)
    for j in range(LANES):
        p[2 * j, j] = 1.0
        p[2 * j + 1, LANES + j] = 1.0
    return jnp.asarray(p, dtype=BF16)


def _experts(xs, blk_e, blk_nv, layer, w_gu_all, b_gu, w_dn_all, b_dn):
    n_rows, d = xs.shape
    _, n_e, _, d_ff2 = w_gu_all.shape
    d_ff = d_ff2 // 2
    tm = EXPERT_ROWS
    nt = d_ff2 // MXU_DIM
    w_gu = w_gu_all.reshape(-1, d, d_ff2)
    w_dn = w_dn_all.reshape(-1, d_ff, d)
    base = layer * n_e
    bgu_p = b_gu.reshape(n_e, nt, LANES, 2).transpose(0, 1, 3, 2).reshape(n_e, 1, d_ff2)
    grid_spec = pltpu.PrefetchScalarGridSpec(
        num_scalar_prefetch=2,
        grid=(n_rows // tm,),
        in_specs=[pl.BlockSpec((tm, d), lambda b, be, nv: (b, 0)),
                  pl.BlockSpec((1, d, d_ff2), lambda b, be, nv: (base + be[b], 0, 0)),
                  pl.BlockSpec((1, 1, d_ff2), lambda b, be, nv: (be[b], 0, 0)),
                  pl.BlockSpec((1, d_ff, d), lambda b, be, nv: (base + be[b], 0, 0)),
                  pl.BlockSpec((1, 1, d), lambda b, be, nv: (be[b], 0, 0)),
                  pl.BlockSpec((MXU_DIM, MXU_DIM), lambda b, be, nv: (0, 0))],
        out_specs=pl.BlockSpec((tm, d), lambda b, be, nv: (b, 0)),
        scratch_shapes=[pltpu.VMEM((d, d_ff2), BF16), pltpu.VMEM((d_ff, d), BF16)],
    )
    return pl.pallas_call(
        _expert_kernel,
        grid_spec=grid_spec,
        out_shape=jax.ShapeDtypeStruct((n_rows, d), F32),
        compiler_params=_params("arbitrary"),
        name="moe_experts",
    )(blk_e, blk_nv, xs, w_gu, bgu_p, w_dn, b_dn.reshape(n_e, 1, d), _deinterleave_matrix())


def _combine_kernel(dest_ref, r_ref, meta_ref, p_ref, wple_ref, gple_ref, wgate_ref, ys_ref, o_ref,
                    buf, sem):
    tm = r_ref.shape[0]

    def row_copy(t, k):
        d = dest_ref[t * TOP_K + k]
        return pltpu.make_async_copy(ys_ref.at[pl.ds(d, 1), :], buf.at[k, pl.ds(t, 1), :], sem)

    def start(t, carry):
        for k in range(TOP_K):
            row_copy(t, k).start()
        return carry

    def wait(t, carry):
        for k in range(TOP_K):
            row_copy(t, k).wait()
        return carry

    lax.fori_loop(0, tm, start, 0)
    lax.fori_loop(0, tm, wait, 0)

    acc = r_ref[...]
    meta = meta_ref[...]
    for k in range(TOP_K):
        acc = acc + meta[:, TOP_K + k:TOP_K + k + 1] * buf[k]
    gate = jax.nn.sigmoid(_dot((_rms_hat(acc) * gple_ref[...]).astype(BF16), wgate_ref[...]))
    o_ref[...] = acc + _dot(p_ref[...].astype(BF16), wple_ref[...]) * gate


def _combine(r, meta, dest_flat, ys, p, w_ple, g_ple, w_gate):
    m, d = r.shape
    pd = p.shape[1]
    tm = 256
    row = lambda n: pl.BlockSpec((tm, n), lambda i: (i, 0))
    return pl.pallas_call(
        _combine_kernel,
        grid=(m // tm,),
        in_specs=[pl.BlockSpec((tm * TOP_K,), lambda i: (i,), memory_space=pltpu.SMEM),
                  row(d), row(LANES), row(pd), _full((pd, d)), _full((1, d)), _full((d, d)),
                  pl.BlockSpec(memory_space=pl.ANY)],
        out_specs=row(d),
        out_shape=jax.ShapeDtypeStruct((m, d), F32),
        scratch_shapes=[pltpu.VMEM((TOP_K, tm, d), F32), pltpu.SemaphoreType.DMA(())],
        compiler_params=_params("arbitrary"),
        name="moe_combine_ple",
    )(dest_flat, r, meta, p, w_ple.astype(BF16), g_ple.reshape(1, d), w_gate.astype(BF16), ys)


def _moe_ple(r, g_ffn, r_w, r_b, layer, w_gu, b_gu, w_dn, b_dn, p, w_ple, g_ple, w_gate):
    m, d = r.shape
    xn, meta, cnt = _router(r, g_ffn, r_w, r_b)
    top_i = meta[:, 0:TOP_K].astype(jnp.int32)
    rank = meta[:, 2 * TOP_K:3 * TOP_K].astype(jnp.int32)
    counts = cnt[0, :N_EXPERTS].astype(jnp.int32)
    padded = ((counts + EXPERT_ROWS - 1) // EXPERT_ROWS) * EXPERT_ROWS
    pad_end = jnp.cumsum(padded)
    pad_start = pad_end - padded
    dest = (pad_start[top_i] + rank).reshape(-1)
    n_rows = m * TOP_K + N_EXPERTS * EXPERT_ROWS
    n_blk = n_rows // EXPERT_ROWS
    blk_start = jnp.arange(n_blk, dtype=jnp.int32) * EXPERT_ROWS
    blk_e = jnp.sum((pad_end[None, :] <= blk_start[:, None]).astype(jnp.int32), axis=1)
    blk_e = jnp.minimum(blk_e, N_EXPERTS - 1)
    blk_nv = jnp.clip(counts[blk_e] - (blk_start - pad_start[blk_e]), 0, EXPERT_ROWS).astype(jnp.int32)
    xs = _dispatch(xn, dest, n_rows)
    ys = _experts(xs, blk_e, blk_nv, layer, w_gu, b_gu, w_dn, b_dn)
    return _combine(r, meta, dest, ys, p, w_ple, g_ple, w_gate)


def _seg_norm(x, bd, gain):
    w = x.shape[1]
    hi, lo = _split2(x * x)
    parts = []
    for c in range(w // MXU_DIM):
        cs = slice(c * MXU_DIM, (c + 1) * MXU_DIM)
        parts.append(_dot(hi[:, cs], bd) + _dot(lo[:, cs], bd))
    ss = parts[0] if len(parts) == 1 else jnp.concatenate(parts, axis=1)
    return x * lax.rsqrt(ss * (1.0 / ATT_HEAD_DIM) + NORM_EPS) * gain


def _rope64(x, cos, sin_signed):
    w = x.shape[1]
    half = ATT_HEAD_DIM // 2
    reps = w // LANES
    if reps > 1:
        cos = jnp.concatenate([cos] * reps, axis=1)
        sin_signed = jnp.concatenate([sin_signed] * reps, axis=1)
    lane = lax.broadcasted_iota(jnp.int32, x.shape, 1)
    first = (lane % ATT_HEAD_DIM) < half
    partner = jnp.where(first, pltpu.roll(x, w - half, axis=1), pltpu.roll(x, half, axis=1))
    return x * cos + partner * sin_signed


def _qkv_kernel(r_ref, gq_ref, gkv_ref, qw_ref, gw_ref, kvw_ref, qn_ref, kn_ref, cos_ref, sin_ref, bd_ref,
                qt_ref, gates_ref, kcv_ref, ks_ref, kw_ref, vst_ref, vwt_ref):
    xhat = _rms_hat(r_ref[...])
    hq = (xhat * gq_ref[...]).astype(BF16)
    hkv = (xhat * gkv_ref[...]).astype(BF16)
    cos = cos_ref[...]
    sin = sin_ref[...]
    bd = bd_ref[...]
    gw = ATT_KV_GROUPS * ATT_HEAD_DIM

    q = _dot(hq, qw_ref[...])
    q = _rope64(_seg_norm(q, bd, qn_ref[...]), cos, sin) * (ATT_HEAD_DIM ** -0.5 * LOG2E)
    qt_ref[0] = q.T.reshape(ATT_HEADS, ATT_HEAD_DIM, q.shape[0]).astype(BF16)
    gates_ref[...] = jax.nn.sigmoid(_dot(hq, gw_ref[...]))

    kv = _dot(hkv, kvw_ref[...])
    kcv_ref[...] = kv[:, 0:2 * gw]
    ks = _rope64(_seg_norm(kv[:, 2 * gw:3 * gw], bd, kn_ref[0:1, :]), cos, sin)
    kw = _rope64(_seg_norm(kv[:, 4 * gw:5 * gw], bd, kn_ref[1:2, :]), cos, sin)
    for g in range(ATT_KV_GROUPS):
        gs = slice(g * ATT_HEAD_DIM, (g + 1) * ATT_HEAD_DIM)
        ks_ref[0, g] = ks[:, gs].astype(BF16)
        kw_ref[0, g] = kw[:, gs].astype(BF16)
    vst = kv[:, 3 * gw:4 * gw].T.astype(BF16)
    vwt = kv[:, 5 * gw:6 * gw].T.astype(BF16)
    ones = jnp.ones((V_ROWS - ATT_HEAD_DIM, vst.shape[1]), BF16)
    for g in range(ATT_KV_GROUPS):
        gs = slice(g * ATT_HEAD_DIM, (g + 1) * ATT_HEAD_DIM)
        vst_ref[0, g, 0:ATT_HEAD_DIM, :] = vst[gs, :]
        vwt_ref[0, g, 0:ATT_HEAD_DIM, :] = vwt[gs, :]
        vst_ref[0, g, ATT_HEAD_DIM:V_ROWS, :] = ones
        vwt_ref[0, g, ATT_HEAD_DIM:V_ROWS, :] = ones


def _rope_tables(pos):
    dh = ATT_HEAD_DIM
    inv = 1.0 / (ROPE_THETA ** (jnp.arange(0, dh, 2, dtype=F32) / dh))
    ang = pos.astype(F32)[:, None] * inv
    c, s = jnp.cos(ang), jnp.sin(ang)
    cos = jnp.concatenate([c, c, c, c], axis=1)
    sin = jnp.concatenate([-s, s, -s, s], axis=1)
    return cos, sin


def _block_diag_ones():
    b = np.kron(np.eye(MXU_DIM // ATT_HEAD_DIM, dtype=np.float32),
                np.ones((ATT_HEAD_DIM, ATT_HEAD_DIM), np.float32))
    return jnp.asarray(b, dtype=BF16)


def _qkv(r, batch, g_q, g_kv, q_w, kv_w, q_norm, k_norm):
    m, d = r.shape
    seq = m // batch
    tm = 256
    nsb = seq // tm
    hd = ATT_HEADS * ATT_HEAD_DIM
    gw = ATT_KV_GROUPS * ATT_HEAD_DIM
    cos, sin = _rope_tables(jnp.arange(seq))
    qw = q_w[:, :hd].astype(BF16)
    gwt = jnp.pad(q_w[:, hd:], ((0, 0), (0, LANES - N_BRANCH * ATT_HEADS))).astype(BF16)
    qn = jnp.tile(q_norm, ATT_HEADS).reshape(1, hd)
    kn = jnp.stack([jnp.tile(k_norm[1], ATT_KV_GROUPS), jnp.tile(k_norm[2], ATT_KV_GROUPS)])
    row = lambda n: pl.BlockSpec((tm, n), lambda i: (i, 0))
    tab = pl.BlockSpec((tm, LANES), lambda i: (i % nsb, 0))
    vt = pl.BlockSpec((1, ATT_KV_GROUPS, V_ROWS, tm), lambda i: (i // nsb, 0, 0, i % nsb))
    kn_spec = pl.BlockSpec((1, ATT_KV_GROUPS, tm, ATT_HEAD_DIM), lambda i: (i // nsb, 0, i % nsb, 0))
    return pl.pallas_call(
        _qkv_kernel,
        grid=(m // tm,),
        in_specs=[row(d), _full((1, d)), _full((1, d)), _full((d, hd)), _full((d, LANES)),
                  _full((d, 6 * gw)), _full((1, hd)), _full((2, gw)), tab, tab,
                  _full((MXU_DIM, MXU_DIM))],
        out_specs=[pl.BlockSpec((1, ATT_HEADS, ATT_HEAD_DIM, tm), lambda i: (i // nsb, 0, 0, i % nsb)),
                   row(LANES), row(2 * gw), kn_spec, kn_spec, vt, vt],
        out_shape=[jax.ShapeDtypeStruct((batch, ATT_HEADS, ATT_HEAD_DIM, seq), BF16),
                   jax.ShapeDtypeStruct((m, LANES), F32),
                   jax.ShapeDtypeStruct((m, 2 * gw), F32),
                   jax.ShapeDtypeStruct((batch, ATT_KV_GROUPS, seq, ATT_HEAD_DIM), BF16),
                   jax.ShapeDtypeStruct((batch, ATT_KV_GROUPS, seq, ATT_HEAD_DIM), BF16),
                   jax.ShapeDtypeStruct((batch, ATT_KV_GROUPS, V_ROWS, seq), BF16),
                   jax.ShapeDtypeStruct((batch, ATT_KV_GROUPS, V_ROWS, seq), BF16)],
        compiler_params=_params("parallel"),
        name="nsa_qkv",
    )(r, g_q.reshape(1, d), g_kv.reshape(1, d), qw, gwt, kv_w.astype(BF16), qn, kn, cos, sin,
      _block_diag_ones())


def _cmp_kernel(x_ref, pe_ref, w1_ref, w2_ref, kn_ref, cos_ref, sin_ref, o_ref, ot_ref):
    nh = x_ref.shape[1]
    half_w = x_ref.shape[2]
    x = x_ref[0]
    h1 = _dot((x + pe_ref[0, 0:1, :]).astype(BF16), w1_ref[0, 0:half_w, :])
    h2 = _dot((x + pe_ref[0, 1:2, :]).astype(BF16), w1_ref[0, half_w:2 * half_w, :])
    pre = h1 + pltpu.roll(h2, nh - 1, axis=0)
    c = _dot(_silu(pre).astype(BF16), w2_ref[0])
    is_k = (pl.program_id(0) // ATT_KV_GROUPS) % 2 == 0
    cn = c * lax.rsqrt(jnp.sum(c * c, axis=-1, keepdims=True) * (1.0 / ATT_HEAD_DIM) + NORM_EPS) * kn_ref[...]
    cn = _rope64(cn, cos_ref[...], sin_ref[...])
    res = jnp.where(is_k, cn, c)
    rowi = lax.broadcasted_iota(jnp.int32, res.shape, 0)
    res = jnp.where(rowi < nh - 1, res, 0.0)
    o_ref[0] = res[:, 0:ATT_HEAD_DIM].astype(BF16)
    ot_ref[0] = res.T[0:ATT_HEAD_DIM, :].astype(BF16)


def _compress(kcv, batch, cmp_pe, cmp_w1, cmp_w2, k_norm0):
    m = kcv.shape[0]
    seq = m // batch
    nh = seq // CMP_STRIDE
    dh = ATT_HEAD_DIM
    half_w = CMP_STRIDE * dh
    hidden = cmp_w1.shape[2]
    x = kcv.reshape(batch, seq, 2, ATT_KV_GROUPS, dh).transpose(0, 2, 3, 1, 4)
    x = x.reshape(batch * 2 * ATT_KV_GROUPS, nh, half_w)
    pe = cmp_pe.reshape(2, 2, half_w)
    w2 = jnp.pad(cmp_w2, ((0, 0), (0, 0), (0, LANES - dh))).astype(BF16)
    kn = jnp.pad(k_norm0, (0, LANES - dh)).reshape(1, LANES)
    cos, sin = _rope_tables(jnp.arange(nh) * CMP_STRIDE + (CMP_BLOCK - 1))
    nidx = lambda i: (i // ATT_KV_GROUPS) % 2
    n_all = batch * 2 * ATT_KV_GROUPS
    c, ct = pl.pallas_call(
        _cmp_kernel,
        grid=(n_all,),
        in_specs=[pl.BlockSpec((1, nh, half_w), lambda i: (i, 0, 0)),
                  pl.BlockSpec((1, 2, half_w), lambda i: (nidx(i), 0, 0)),
                  pl.BlockSpec((1, 2 * half_w, hidden), lambda i: (nidx(i), 0, 0)),
                  pl.BlockSpec((1, hidden, LANES), lambda i: (nidx(i), 0, 0)),
                  _full((1, LANES)), _full((nh, LANES)), _full((nh, LANES))],
        out_specs=[pl.BlockSpec((1, nh, dh), lambda i: (i, 0, 0)),
                   pl.BlockSpec((1, dh, nh), lambda i: (i, 0, 0))],
        out_shape=[jax.ShapeDtypeStruct((n_all, nh, dh), BF16),
                   jax.ShapeDtypeStruct((n_all, dh, nh), BF16)],
        compiler_params=_params("parallel"),
        name="nsa_compress",
    )(x, pe, cmp_w1.astype(BF16), w2, kn, cos, sin)
    return (c.reshape(batch, 2, ATT_KV_GROUPS, nh, dh), ct.reshape(batch, 2, ATT_KV_GROUPS, dh, nh))


def _attn_kernel(qt_ref, gates_ref, kc_ref, vct_ref, ks_ref, vst_ref, kw_ref, vwt_ref, selwt_ref,
                 o_ref, s_ref, ps_ref, sel_ref, m_ref, acc_ref, oc_ref, ow_ref,
                 sa_ref, sb_ref, mxa_ref, mxb_ref, sw_ref):
    i = pl.program_id(2)
    n_cmp = kc_ref.shape[-2]
    n_sel = selwt_ref.shape[0]
    t0 = i * Q_BLOCK
    t_row = t0 + lax.broadcasted_iota(jnp.int32, (1, Q_BLOCK), 1)
    heads = range(ATT_REP)

    k_c = kc_ref[0, 0, 0]
    vt_c = vct_ref[0, 0, 0]
    cend = lax.broadcasted_iota(jnp.int32, (n_cmp, 1), 0) * CMP_STRIDE + (CMP_BLOCK - 1)
    ok_c = cend <= t_row
    for r in heads:
        s_ref[r, 0:n_cmp, :] = jnp.where(ok_c, _dot(k_c, qt_ref[0, r]), NEG)
    for r in heads:
        s = s_ref[r, 0:n_cmp, :]
        p = jnp.where(ok_c, jnp.exp2(s - jnp.max(s, axis=0, keepdims=True)), 0.0)
        den = jnp.sum(p, axis=0, keepdims=True)
        pc = p * (1.0 / jnp.where(den > 0, den, 1.0))
        oc_ref[r] = _dot(vt_c, pc.astype(BF16))
        if r == 0:
            ps_ref[...] = pc
        else:
            ps_ref[...] = ps_ref[...] + pc

    ps_hi, ps_mid, ps_lo = _split3(ps_ref[...])
    selwt = selwt_ref[...]
    imp = _dot(selwt, ps_hi) + _dot(selwt, ps_mid) + _dot(selwt, ps_lo)
    j = lax.broadcasted_iota(jnp.int32, (n_sel, Q_BLOCK), 0)
    jf = j.astype(F32)
    cur = t_row // SEL_BLOCK
    valid = j <= cur
    forced = (j == 0) | (j == cur) | (j == cur - 1)
    sc = jnp.where(forced, -jnp.inf, jnp.where(valid, imp, -1.0))
    sel = jnp.where(forced, 1.0, 0.0)
    for _ in range(min(SEL_TOPK, n_sel) - N_FORCED):
        mx = jnp.max(sc, axis=0, keepdims=True)
        ixf = jnp.min(jnp.where(sc == mx, jf, float(n_sel)), axis=0, keepdims=True)
        pick = jf == ixf
        sel = jnp.where(pick, 1.0, sel)
        sc = jnp.where(pick, -jnp.inf, sc)

    wk = WINDOW + Q_BLOCK
    st = pl.multiple_of(jnp.maximum(t0 - WINDOW, 0), LANES)
    k_w = kw_ref[0, 0, pl.ds(st, wk), :]
    vt_w = vwt_ref[0, 0, :, pl.ds(st, wk)]
    kpos_w = st + lax.broadcasted_iota(jnp.int32, (wk, 1), 0)
    ok_w = (kpos_w <= t_row) & (kpos_w > t_row - WINDOW)
    for r in heads:
        sw_ref[r] = jnp.where(ok_w, _dot(k_w, qt_ref[0, r]), NEG)
    for r in heads:
        s = sw_ref[r]
        p = jnp.exp2(s - jnp.max(s, axis=0, keepdims=True))
        res = _dot(vt_w, p.astype(BF16))
        ow_ref[r] = res[0:ATT_HEAD_DIM] / res[ATT_HEAD_DIM:ATT_HEAD_DIM + 1]

    first_own = t0 // SEL_BLOCK
    sel_ref[...] = jnp.where(valid & (j < first_own), sel, 0.0)

    q_all = jnp.concatenate([qt_ref[0, r] for r in heads], axis=1)
    own = pl.multiple_of(t0, Q_BLOCK)
    vt_o = vst_ref[0, 0, :, pl.ds(own, Q_BLOCK)]
    s_o = _dot(ks_ref[0, 0, pl.ds(own, Q_BLOCK), :], q_all)
    tri = (lax.broadcasted_iota(jnp.int32, (Q_BLOCK, 1), 0)
           <= lax.broadcasted_iota(jnp.int32, (1, Q_BLOCK), 1))
    for r in heads:
        s = jnp.where(tri, s_o[:, r * Q_BLOCK:(r + 1) * Q_BLOCK], NEG)
        mx = jnp.max(s, axis=0, keepdims=True)
        m_ref[r:r + 1, :] = mx
        acc_ref[r] = _dot(vt_o, jnp.exp2(s - mx).astype(BF16))

    blocks = SEL_CHUNK // SEL_BLOCK
    last_chunk = ks_ref.shape[2] // SEL_CHUNK - 1

    def scores(c, buf_ref, mx_ref):
        cc = jnp.minimum(c, last_chunk)
        off = pl.multiple_of(cc * SEL_CHUNK, SEL_CHUNK)
        k_s = ks_ref[0, 0, pl.ds(off, SEL_CHUNK), :]
        picked = sel_ref[pl.ds(pl.multiple_of(cc * blocks, blocks), blocks), :]
        mk = jnp.broadcast_to(picked[:, None, :], (blocks, SEL_BLOCK, Q_BLOCK)).reshape(SEL_CHUNK, Q_BLOCK) > 0.5
        s_all = _dot(k_s, q_all)
        for r in heads:
            s = jnp.where(mk, s_all[:, r * Q_BLOCK:(r + 1) * Q_BLOCK], NEG)
            buf_ref[r] = s
            mx_ref[r:r + 1, :] = jnp.max(s, axis=0, keepdims=True)

    def accumulate(c, buf_ref, mx_ref):
        cc = jnp.minimum(c, last_chunk)
        off = pl.multiple_of(cc * SEL_CHUNK, SEL_CHUNK)
        vt_s = vst_ref[0, 0, :, pl.ds(off, SEL_CHUNK)]
        for r in heads:
            m_old = m_ref[r:r + 1, :]
            m_new = jnp.maximum(m_old, mx_ref[r:r + 1, :])
            p = jnp.exp2(buf_ref[r] - m_new)
            acc_ref[r] = jnp.exp2(m_old - m_new) * acc_ref[r] + _dot(vt_s, p.astype(BF16))
            m_ref[r:r + 1, :] = m_new

    n_chunks = (first_own + blocks - 1) // blocks
    scores(0, sa_ref, mxa_ref)

    def body(k, carry):
        scores(2 * k + 1, sb_ref, mxb_ref)
        accumulate(2 * k, sa_ref, mxa_ref)
        scores(2 * k + 2, sa_ref, mxa_ref)
        accumulate(2 * k + 1, sb_ref, mxb_ref)
        return carry

    lax.fori_loop(0, (n_chunks + 1) // 2, body, 0)

    gts = gates_ref[0, 0, 0]
    outs = []
    for r in heads:
        g = [gts[r * N_BRANCH + b:r * N_BRANCH + b + 1, :] for b in range(N_BRANCH)]
        o_sel = acc_ref[r, 0:ATT_HEAD_DIM, :] / acc_ref[r, ATT_HEAD_DIM:ATT_HEAD_DIM + 1, :]
        outs.append(g[0] * oc_ref[r] + g[1] * o_sel + g[2] * ow_ref[r])
    for pr in range(ATT_REP // 2):
        pair = jnp.concatenate([outs[2 * pr], outs[2 * pr + 1]], axis=0)
        o_ref[0, :, pr * LANES:(pr + 1) * LANES] = pair.T


def _cmp_to_sel_weights(seq):
    n_cmp = (seq - CMP_BLOCK) // CMP_STRIDE + 1
    n_sel = seq // SEL_BLOCK
    cs = np.arange(n_cmp) * CMP_STRIDE
    ss = np.arange(n_sel) * SEL_BLOCK
    ov = np.minimum(cs[:, None] + CMP_BLOCK, ss[None, :] + SEL_BLOCK) - np.maximum(cs[:, None], ss[None, :])
    w = np.clip(ov, 0, None).astype(np.float32) / CMP_BLOCK
    w = np.concatenate([w, np.zeros((seq // CMP_STRIDE - n_cmp, n_sel), np.float32)], axis=0)
    return jnp.asarray(w.T, dtype=BF16)


def _attention(qt, gates, kc, kct, ks, kw, vst, vwt, batch):
    seq = qt.shape[3]
    nqb = seq // Q_BLOCK
    dh = ATT_HEAD_DIM
    n_cmp = seq // CMP_STRIDE
    n_sel = seq // SEL_BLOCK
    hd = ATT_HEADS * dh
    gw = ATT_KV_GROUPS * dh
    wk = WINDOW + Q_BLOCK
    g3 = gates[:, :N_BRANCH * ATT_HEADS].reshape(batch, nqb, Q_BLOCK, ATT_KV_GROUPS, ATT_REP * N_BRANCH)
    g3 = g3.transpose(0, 3, 1, 4, 2)
    g3 = jnp.pad(g3, ((0, 0), (0, 0), (0, 0), (0, GATE_ROWS - ATT_REP * N_BRANCH), (0, 0)))
    per_bg = lambda *shape: pl.BlockSpec((1, 1) + shape, lambda b, g, i: (b, g, 0, 0))
    head_tile = pltpu.VMEM((ATT_REP, dh, Q_BLOCK), F32)
    return pl.pallas_call(
        _attn_kernel,
        grid=(batch, ATT_KV_GROUPS, nqb),
        in_specs=[pl.BlockSpec((1, ATT_REP, dh, Q_BLOCK), lambda b, g, i: (b, g, 0, i)),
                  pl.BlockSpec((1, 1, 1, GATE_ROWS, Q_BLOCK), lambda b, g, i: (b, g, i, 0, 0)),
                  pl.BlockSpec((1, 1, 1, n_cmp, dh), lambda b, g, i: (b, 0, g, 0, 0)),
                  pl.BlockSpec((1, 1, 1, dh, n_cmp), lambda b, g, i: (b, 1, g, 0, 0)),
                  per_bg(seq, dh), per_bg(V_ROWS, seq), per_bg(seq, dh), per_bg(V_ROWS, seq),
                  pl.BlockSpec((n_sel, n_cmp), lambda b, g, i: (0, 0))],
        out_specs=pl.BlockSpec((1, Q_BLOCK, gw), lambda b, g, i: (b, i, g)),
        out_shape=jax.ShapeDtypeStruct((batch, seq, hd), F32),
        scratch_shapes=[pltpu.VMEM((ATT_REP, n_cmp, Q_BLOCK), F32),
                        pltpu.VMEM((n_cmp, Q_BLOCK), F32), pltpu.VMEM((n_sel, Q_BLOCK), F32),
                        pltpu.VMEM((SUBLANES, Q_BLOCK), F32),
                        pltpu.VMEM((ATT_REP, V_ROWS, Q_BLOCK), F32), head_tile, head_tile,
                        pltpu.VMEM((ATT_REP, SEL_CHUNK, Q_BLOCK), F32), pltpu.VMEM((ATT_REP, SEL_CHUNK, Q_BLOCK), F32),
                        pltpu.VMEM((SUBLANES, Q_BLOCK), F32), pltpu.VMEM((SUBLANES, Q_BLOCK), F32),
                        pltpu.VMEM((ATT_REP, wk, Q_BLOCK), F32)],
        compiler_params=_params("parallel", "parallel", "arbitrary"),
        name="nsa_attention",
    )(qt, g3, kc, kct, ks, vst, kw, vwt, _cmp_to_sel_weights(seq))


def _oproj_kernel(a_ref, r_ref, w_ref, o_ref):
    o_ref[...] = r_ref[...] + _dot(a_ref[...].astype(BF16), w_ref[...])


def _oproj(a, r, w):
    m, d = r.shape
    k = a.shape[1]
    tm = 512
    row = lambda n: pl.BlockSpec((tm, n), lambda i: (i, 0))
    return pl.pallas_call(
        _oproj_kernel,
        grid=(m // tm,),
        in_specs=[row(k), row(d), _full((k, d))],
        out_specs=row(d),
        out_shape=jax.ShapeDtypeStruct((m, d), F32),
        compiler_params=_params("parallel"),
        name="nsa_oproj",
    )(a, r, w.astype(BF16))


def kernel(x, p, g_mix, g_ffn, m_w_in, m_conv_w, m_conv_b, m_dt_bias, m_a_log, m_d, m_g_norm, m_w_out,
           kv_g, kv_w, cmp_pe, cmp_w1, cmp_w2, k_norm, q_w, q_norm, o_w,
           r_w, r_b, e_w_gu, e_b_gu, e_w_dn, e_b_dn, ple_w, ple_g, ple_gate_w):
    batch, seq, d = x.shape
    m = batch * seq
    depth = p.shape[0]
    n_a = m_w_in.shape[0]
    r = x.reshape(m, d)
    pf = p.reshape(depth, m, p.shape[-1])
    shared = None
    for i in range(depth):
        if i < n_a:
            z, xbc, dt = _inproj(r, g_mix[i], m_w_in[i])
            r = _ssd(z, xbc, dt, r, batch, m_conv_w[i], m_conv_b[i], m_dt_bias[i], m_a_log[i], m_d[i],
                     m_g_norm[i], m_w_out[i])
        else:
            jb = i - n_a
            qt, gates, kcv, ks, kw, vst, vwt = _qkv(r, batch, g_mix[i], kv_g, q_w[jb], kv_w, q_norm[jb], k_norm)
            if shared is None:
                shared = _compress(kcv, batch, cmp_pe, cmp_w1, cmp_w2, k_norm[0]) + (ks, kw, vst, vwt)
            a = _attention(qt, gates, *shared, batch)
            r = _oproj(a.reshape(m, -1), r, o_w[jb])
        r = _moe_ple(r, g_ffn[i], r_w[i], r_b[i], i, e_w_gu, e_b_gu[i], e_w_dn, e_b_dn[i],
                     pf[i], ple_w[i], ple_g[i], ple_gate_w[i])
    return r.reshape(batch, seq, d)
```

```python
import functools
import math

import numpy as np
import jax
import jax.numpy as jnp
from jax import lax
from jax.experimental import pallas as pl
from jax.experimental.pallas import tpu as pltpu

F32 = jnp.float32
BF16 = jnp.bfloat16

NORM_EPS = 1e-6
ROPE_THETA = 10000.0
SSM_HEADS = 32
SSM_HEAD_DIM = 64
SSM_GROUPS = 4
SSM_STATE = 128
SSM_CONV = 4
SSM_CHUNK = 256
SSM_D_INNER = SSM_HEADS * SSM_HEAD_DIM
SSM_GN = SSM_GROUPS * SSM_STATE
SSM_CONV_DIM = SSM_D_INNER + 2 * SSM_GN
SSM_GROUP_W = SSM_D_INNER // SSM_GROUPS
ATT_HEAD_DIM = 64
ATT_KV_GROUPS = 4
ATT_REP = 4
ATT_HEADS = ATT_KV_GROUPS * ATT_REP
N_BRANCH = 3
CMP_BLOCK = 32
CMP_STRIDE = 16
SEL_BLOCK = 64
SEL_TOPK = 16
WINDOW = 512
Q_BLOCK = 128
SEL_FORCE = 1000.0
N_FORCED = 3
SEL_CHUNK = 512
GATE_ROWS = 16
V_ROWS = 80
LOG2E = 1.4426950408889634
N_EXPERTS = 32
TOP_K = 4
SWIGLU_LIMIT = 7.0
SWIGLU_ALPHA = 1.702
EXPERT_ROWS = 256

LANES = 128
SUBLANES = 8
MXU_DIM = 256
VMEM_LIMIT_BYTES = 56 * 1024 * 1024
NEG = -1e30


def _params(*sem):
    return pltpu.CompilerParams(dimension_semantics=sem, vmem_limit_bytes=VMEM_LIMIT_BYTES)


def _dot(a, b):
    return jnp.dot(a, b, preferred_element_type=F32)


def _dot_nt(a, b):
    return lax.dot_general(a, b, (((1,), (1,)), ((), ())), preferred_element_type=F32)


def _split2(x):
    hi = x.astype(BF16)
    lo = (x - hi.astype(F32)).astype(BF16)
    return hi, lo


def _split3(x):
    hi = x.astype(BF16)
    r = x - hi.astype(F32)
    mid = r.astype(BF16)
    lo = (r - mid.astype(F32)).astype(BF16)
    return hi, mid, lo


def _dot_parts(parts, m):
    out = _dot(parts[0], m)
    for p in parts[1:]:
        out = out + _dot(p, m)
    return out


def _rms_hat(x):
    return x * lax.rsqrt(jnp.mean(x * x, axis=-1, keepdims=True) + NORM_EPS)


def _silu(x):
    return x * jax.nn.sigmoid(x)


def _full(shape):
    nd = len(shape)
    return pl.BlockSpec(shape, lambda *_: (0,) * nd)


def _inproj_kernel(x_ref, g_ref, wz_ref, wx_ref, wdh_ref, wdl_ref, z_ref, xbc_ref, dt_ref):
    h = _rms_hat(x_ref[...]) * g_ref[...]
    hb, hl = _split2(h)
    z_ref[...] = _dot(hb, wz_ref[...])
    xbc_ref[...] = _dot(hb, wx_ref[...])
    dt_ref[...] = _dot(hb, wdh_ref[...]) + _dot(hl, wdh_ref[...]) + _dot(hb, wdl_ref[...])


def _inproj(x, g, w_in):
    m, d = x.shape
    tm = 256
    wz = w_in[:, :SSM_D_INNER].astype(BF16)
    wx = w_in[:, SSM_D_INNER:SSM_D_INNER + SSM_CONV_DIM].astype(BF16)
    wd = jnp.pad(w_in[:, SSM_D_INNER + SSM_CONV_DIM:], ((0, 0), (0, LANES - SSM_HEADS)))
    wdh, wdl = _split2(wd)
    row = lambda n: pl.BlockSpec((tm, n), lambda i: (i, 0))
    return pl.pallas_call(
        _inproj_kernel,
        grid=(m // tm,),
        in_specs=[row(d), _full((1, d)), _full((d, SSM_D_INNER)), _full((d, SSM_CONV_DIM)),
                  _full((d, LANES)), _full((d, LANES))],
        out_specs=[row(SSM_D_INNER), row(SSM_CONV_DIM), row(LANES)],
        out_shape=[jax.ShapeDtypeStruct((m, SSM_D_INNER), F32),
                   jax.ShapeDtypeStruct((m, SSM_CONV_DIM), F32),
                   jax.ShapeDtypeStruct((m, LANES), F32)],
        compiler_params=_params("parallel"),
        name="ssm_inproj",
    )(x, g.reshape(1, d), wz, wx, wdh, wdl)


def _ssd_kernel(z_ref, xbc_ref, dt_ref, x_ref, cw_ref, cb_ref, dtb_ref, alog_ref, dsk_ref, gn_ref,
                wout_ref, exp_ref, o_ref, state_ref, ext_ref, y_ref):
    L = SSM_CHUNK
    c = pl.program_id(1)

    @pl.when(c == 0)
    def _():
        state_ref[...] = jnp.zeros_like(state_ref)
        ext_ref[0:SUBLANES, :] = jnp.zeros((SUBLANES, SSM_CONV_DIM), F32)

    ext_ref[SUBLANES:SUBLANES + L, :] = xbc_ref[...]
    acc = jnp.broadcast_to(cb_ref[...], (L, SSM_CONV_DIM))
    for k in range(SSM_CONV):
        off = SUBLANES - (SSM_CONV - 1) + k
        acc = acc + cw_ref[k:k + 1, :] * ext_ref[off:off + L, :]
    ext_ref[0:SUBLANES, :] = ext_ref[L:L + SUBLANES, :]
    xc = _silu(acc)
    xs = xc[:, :SSM_D_INNER]
    bm = xc[:, SSM_D_INNER:SSM_D_INNER + SSM_GN]
    cm = xc[:, SSM_D_INNER + SSM_GN:]

    dtr = dt_ref[...] + dtb_ref[...]
    dt = jnp.maximum(dtr, 0.0) + jnp.log(1.0 + jnp.exp(-jnp.abs(dtr)))
    a_neg = -jnp.exp(alog_ref[...])
    da = dt * a_neg
    row = lax.broadcasted_iota(jnp.int32, (L, L), 0)
    col = lax.broadcasted_iota(jnp.int32, (L, L), 1)
    causal = row >= col
    tri = jnp.where(causal, 1.0, 0.0).astype(BF16)
    da_hi, da_mid, da_lo = _split3(da)
    acs = _dot(tri, da_hi) + _dot(tri, da_mid) + _dot(tri, da_lo)
    a_last = acs[L - 1:L, :]
    d_end = jnp.exp(a_last - acs)
    d_start = jnp.exp(acs)
    cdec = jnp.broadcast_to(jnp.exp(a_last), (SUBLANES, LANES))

    ex = exp_ref[...]
    dt_e = _dot(dt.astype(BF16), ex)
    d_end_e = _dot(d_end.astype(BF16), ex)
    d_start_e = _dot_parts(_split2(d_start), ex)
    cdec_e = _dot_parts(_split2(cdec), ex)[0:1, :]

    xdt = xs * dt_e
    xb = xdt.astype(BF16)
    xd = (xdt * d_end_e).astype(BF16)
    acs_t = acs.T
    bm_t = bm.T
    bmb = bm.astype(BF16)
    cmb = cm.astype(BF16)

    for g in range(SSM_GROUPS):
        gs = slice(g * SSM_STATE, (g + 1) * SSM_STATE)
        ws = slice(g * SSM_GROUP_W, (g + 1) * SSM_GROUP_W)
        cb = _dot_nt(cmb[:, gs], bmb[:, gs])
        for r in range(SSM_HEADS // SSM_GROUPS):
            h = g * (SSM_HEADS // SSM_GROUPS) + r
            hs = slice(h * SSM_HEAD_DIM, (h + 1) * SSM_HEAD_DIM)
            diff = acs[:, h:h + 1] - acs_t[h:h + 1, :]
            lm = jnp.exp(jnp.where(causal, diff, -jnp.inf))
            y_ref[:, hs] = _dot((cb * lm).astype(BF16), xb[:, hs])
        st = state_ref[g]
        y_ref[:, ws] = y_ref[:, ws] + _dot(cmb[:, gs], st.astype(BF16)) * d_start_e[:, ws]
        state_ref[g] = cdec_e[:, ws] * st + _dot(bm_t[gs, :].astype(BF16), xd[:, ws])

    zz = z_ref[...]
    y = (y_ref[...] + dsk_ref[...] * xs) * _silu(zz)
    for g in range(SSM_GROUPS):
        ws = slice(g * SSM_GROUP_W, (g + 1) * SSM_GROUP_W)
        yg = y[:, ws]
        y_ref[:, ws] = yg * lax.rsqrt(jnp.mean(yg * yg, axis=-1, keepdims=True) + NORM_EPS)
    yn = (y_ref[...] * gn_ref[...]).astype(BF16)
    o_ref[...] = x_ref[...] + _dot(yn, wout_ref[...])


def _head_expand_matrix():
    e = np.zeros((LANES, SSM_D_INNER), np.float32)
    for h in range(SSM_HEADS):
        e[h, h * SSM_HEAD_DIM:(h + 1) * SSM_HEAD_DIM] = 1.0
    return jnp.asarray(e, dtype=BF16)


def _ssd(z, xbc, dt, x, batch, conv_w, conv_b, dt_bias, a_log, d_skip, g_norm, w_out):
    m, d = x.shape
    L = SSM_CHUNK
    nc = m // batch // L
    padh = lambda v: jnp.pad(v, (0, LANES - SSM_HEADS)).reshape(1, LANES)
    row = lambda n: pl.BlockSpec((L, n), lambda b, c: (b * nc + c, 0))
    return pl.pallas_call(
        _ssd_kernel,
        grid=(batch, nc),
        in_specs=[row(SSM_D_INNER), row(SSM_CONV_DIM), row(LANES), row(d),
                  _full((SSM_CONV, SSM_CONV_DIM)), _full((1, SSM_CONV_DIM)), _full((1, LANES)),
                  _full((1, LANES)), _full((1, SSM_D_INNER)), _full((1, SSM_D_INNER)),
                  _full((SSM_D_INNER, d)), _full((LANES, SSM_D_INNER))],
        out_specs=row(d),
        out_shape=jax.ShapeDtypeStruct((m, d), F32),
        scratch_shapes=[pltpu.VMEM((SSM_GROUPS, SSM_STATE, SSM_GROUP_W), F32),
                        pltpu.VMEM((L + 2 * SUBLANES, SSM_CONV_DIM), F32),
                        pltpu.VMEM((L, SSM_D_INNER), F32)],
        compiler_params=_params("parallel", "arbitrary"),
        name="ssm_scan",
    )(z, xbc, dt, x, conv_w.T, conv_b.reshape(1, -1), padh(dt_bias), padh(a_log),
      jnp.repeat(d_skip, SSM_HEAD_DIM).reshape(1, -1), g_norm.reshape(1, -1), w_out.astype(BF16),
      _head_expand_matrix())


def _router_kernel(r_ref, g_ref, rwh_ref, rwl_ref, rb_ref, xn_ref, meta_ref, cnt_ref, carry_ref):
    tm = r_ref.shape[0]
    i = pl.program_id(0)

    @pl.when(i == 0)
    def _():
        carry_ref[...] = jnp.zeros_like(carry_ref)

    xn = _rms_hat(r_ref[...]) * g_ref[...]
    xn_ref[...] = xn
    hb, hl = _split2(xn)
    logits = _dot(hb, rwh_ref[...]) + _dot(hl, rwh_ref[...]) + _dot(hb, rwl_ref[...]) + rb_ref[...]
    lane = lax.broadcasted_iota(jnp.int32, (tm, LANES), 1)
    vals, idxs = [], []
    cur = logits
    for _ in range(TOP_K):
        mx = jnp.max(cur, axis=-1, keepdims=True)
        ix = jnp.min(jnp.where(cur == mx, lane, LANES), axis=-1, keepdims=True)
        vals.append(mx)
        idxs.append(ix)
        cur = jnp.where(lane == ix, -jnp.inf, cur)
    es = [jnp.exp(v - vals[0]) for v in vals]
    den = es[0] + es[1] + es[2] + es[3]
    sel = jnp.zeros((tm, LANES), F32)
    for ix in idxs:
        sel = sel + jnp.where(lane == ix, 1.0, 0.0)
    row = lax.broadcasted_iota(jnp.int32, (tm, tm), 0)
    col = lax.broadcasted_iota(jnp.int32, (tm, tm), 1)
    strict = jnp.where(row > col, 1.0, 0.0).astype(BF16)
    before = _dot(strict, sel.astype(BF16)) + carry_ref[0:1, :]
    meta = jnp.zeros((tm, LANES), F32)
    for k in range(TOP_K):
        rank = jnp.sum(jnp.where(lane == idxs[k], before, 0.0), axis=-1, keepdims=True)
        meta = jnp.where(lane == k, idxs[k].astype(F32), meta)
        meta = jnp.where(lane == TOP_K + k, es[k] / den, meta)
        meta = jnp.where(lane == 2 * TOP_K + k, rank, meta)
    meta_ref[...] = meta
    carry = carry_ref[...] + jnp.sum(sel, axis=0, keepdims=True)
    carry_ref[...] = carry
    cnt_ref[...] = carry


def _router(r, g, r_w, r_b):
    m, d = r.shape
    tm = 512
    rw = jnp.pad(r_w, ((0, 0), (0, LANES - N_EXPERTS)))
    rwh, rwl = _split2(rw)
    rb = jnp.pad(r_b, (0, LANES - N_EXPERTS), constant_values=NEG).reshape(1, LANES)
    row = lambda n: pl.BlockSpec((tm, n), lambda i: (i, 0))
    return pl.pallas_call(
        _router_kernel,
        grid=(m // tm,),
        in_specs=[row(d), _full((1, d)), _full((d, LANES)), _full((d, LANES)), _full((1, LANES))],
        out_specs=[row(d), row(LANES), _full((SUBLANES, LANES))],
        out_shape=[jax.ShapeDtypeStruct((m, d), F32), jax.ShapeDtypeStruct((m, LANES), F32),
                   jax.ShapeDtypeStruct((SUBLANES, LANES), F32)],
        scratch_shapes=[pltpu.VMEM((SUBLANES, LANES), F32)],
        compiler_params=_params("arbitrary"),
        name="moe_router",
    )(r, g.reshape(1, d), rwh, rwl, rb)


def _for_each_row_copy(tm, row_copy):
    def start(tb, carry):
        base = pl.multiple_of(tb * SUBLANES, SUBLANES)
        for j in range(SUBLANES):
            for k in range(TOP_K):
                row_copy(base + j, k).start(priority=k % 2)
        return carry

    def wait(tb, carry):
        base = pl.multiple_of(tb * SUBLANES, SUBLANES)
        for j in range(SUBLANES):
            for k in range(TOP_K):
                row_copy(base + j, k).wait()
        return carry

    lax.fori_loop(0, tm // SUBLANES, start, 0)
    lax.fori_loop(0, tm // SUBLANES, wait, 0)


def _dispatch_kernel(tail_ref, dest_ref, xn_ref, xs_ref, zeros_ref, sem, zsem):
    tm = xn_ref.shape[0]
    rows = zeros_ref.shape[0]

    def tail_copy(e):
        start = pl.multiple_of(jnp.maximum(tail_ref[e], 0), rows)
        return pltpu.make_async_copy(zeros_ref, xs_ref.at[pl.ds(start, rows), :], zsem)

    @pl.when(pl.program_id(0) == 0)
    def _():
        zeros_ref[...] = jnp.zeros_like(zeros_ref)
        for e in range(2 * N_EXPERTS):
            pl.when(tail_ref[e] >= 0)(tail_copy(e).start)
        for e in range(2 * N_EXPERTS):
            pl.when(tail_ref[e] >= 0)(tail_copy(e).wait)

    def row_copy(t, k):
        d = dest_ref[t * TOP_K + k]
        return pltpu.make_async_copy(xn_ref.at[pl.ds(t, 1), :], xs_ref.at[pl.ds(d, 1), :], sem)

    _for_each_row_copy(tm, row_copy)


def _dispatch(xn, dest_flat, tail_start, n_rows):
    m, d = xn.shape
    tm = 256
    grid_spec = pltpu.PrefetchScalarGridSpec(
        num_scalar_prefetch=1,
        grid=(m // tm,),
        in_specs=[pl.BlockSpec((tm * TOP_K,), lambda i, tail: (i,), memory_space=pltpu.SMEM),
                  pl.BlockSpec((tm, d), lambda i, tail: (i, 0))],
        out_specs=pl.BlockSpec(memory_space=pl.ANY),
        scratch_shapes=[pltpu.VMEM((EXPERT_ROWS, d), F32), pltpu.SemaphoreType.DMA(()),
                        pltpu.SemaphoreType.DMA(())],
    )
    return pl.pallas_call(
        _dispatch_kernel,
        grid_spec=grid_spec,
        out_shape=jax.ShapeDtypeStruct((n_rows, d), F32),
        compiler_params=_params("arbitrary"),
        name="moe_dispatch",
    )(tail_start, dest_flat, xn)


def _expert_kernel(be_ref, nv_ref, xs_ref, wgu_ref, bgu_ref, wdn_ref, bdn_ref, perm_ref, ys_ref,
                   wgu_s, wdn_s):
    b = pl.program_id(0)
    e = be_ref[b]
    prev = be_ref[jnp.maximum(b - 1, 0)]
    d_ff2 = wgu_s.shape[1]

    @pl.when(jnp.logical_or(b == 0, e != prev))
    def _():
        for c in range(d_ff2 // MXU_DIM):
            cs = slice(c * MXU_DIM, (c + 1) * MXU_DIM)
            wgu_s[:, cs] = _dot(wgu_ref[0, :, cs].astype(BF16), perm_ref[...]).astype(BF16)
        wdn_s[...] = wdn_ref[0].astype(BF16)

    @pl.when(nv_ref[b] > 0)
    def _():
        h = _dot(xs_ref[...].astype(BF16), wgu_s[...]) + bgu_ref[0]
        acts = []
        for c in range(d_ff2 // MXU_DIM):
            gate = jnp.minimum(h[:, c * MXU_DIM:c * MXU_DIM + LANES], SWIGLU_LIMIT)
            up = jnp.clip(h[:, c * MXU_DIM + LANES:(c + 1) * MXU_DIM], -SWIGLU_LIMIT, SWIGLU_LIMIT)
            acts.append((up + 1.0) * (gate * jax.nn.sigmoid(SWIGLU_ALPHA * gate)))
        a = jnp.concatenate(acts, axis=1).astype(BF16)
        ys_ref[...] = _dot(a, wdn_s[...]) + bdn_ref[0]

    @pl.when(nv_ref[b] == 0)
    def _():
        ys_ref[...] = jnp.zeros_like(ys_ref)


def _deinterleave_matrix():
    p = np.zeros((MXU_DIM, MXU_DIM), np.float32)
    for j in range(LANES):
        p[2 * j, j] = 1.0
        p[2 * j + 1, LANES + j] = 1.0
    return jnp.asarray(p, dtype=BF16)


def _experts(xs, blk_e, blk_nv, layer, w_gu_all, b_gu, w_dn_all, b_dn):
    n_rows, d = xs.shape
    _, n_e, _, d_ff2 = w_gu_all.shape
    d_ff = d_ff2 // 2
    tm = EXPERT_ROWS
    nt = d_ff2 // MXU_DIM
    w_gu = w_gu_all.reshape(-1, d, d_ff2)
    w_dn = w_dn_all.reshape(-1, d_ff, d)
    base = layer * n_e
    bgu_p = b_gu.reshape(n_e, nt, LANES, 2).transpose(0, 1, 3, 2).reshape(n_e, 1, d_ff2)
    grid_spec = pltpu.PrefetchScalarGridSpec(
        num_scalar_prefetch=2,
        grid=(n_rows // tm,),
        in_specs=[pl.BlockSpec((tm, d), lambda b, be, nv: (b, 0)),
                  pl.BlockSpec((1, d, d_ff2), lambda b, be, nv: (base + be[b], 0, 0)),
                  pl.BlockSpec((1, 1, d_ff2), lambda b, be, nv: (be[b], 0, 0)),
                  pl.BlockSpec((1, d_ff, d), lambda b, be, nv: (base + be[b], 0, 0)),
                  pl.BlockSpec((1, 1, d), lambda b, be, nv: (be[b], 0, 0)),
                  pl.BlockSpec((MXU_DIM, MXU_DIM), lambda b, be, nv: (0, 0))],
        out_specs=pl.BlockSpec((tm, d), lambda b, be, nv: (b, 0)),
        scratch_shapes=[pltpu.VMEM((d, d_ff2), BF16), pltpu.VMEM((d_ff, d), BF16)],
    )
    return pl.pallas_call(
        _expert_kernel,
        grid_spec=grid_spec,
        out_shape=jax.ShapeDtypeStruct((n_rows, d), F32),
        compiler_params=_params("arbitrary"),
        name="moe_experts",
    )(blk_e, blk_nv, xs, w_gu, bgu_p, w_dn, b_dn.reshape(n_e, 1, d), _deinterleave_matrix())


def _combine_kernel(dest_ref, r_ref, meta_ref, p_ref, wple_ref, gple_ref, wgate_ref, ys_ref, o_ref,
                    buf, sem):
    tm = r_ref.shape[0]

    def row_copy(t, k):
        d = dest_ref[t * TOP_K + k]
        return pltpu.make_async_copy(ys_ref.at[pl.ds(d, 1), :], buf.at[k, pl.ds(t, 1), :], sem)

    _for_each_row_copy(tm, row_copy)

    acc = r_ref[...]
    meta = meta_ref[...]
    for k in range(TOP_K):
        acc = acc + meta[:, TOP_K + k:TOP_K + k + 1] * buf[k]
    gate = jax.nn.sigmoid(_dot((_rms_hat(acc) * gple_ref[...]).astype(BF16), wgate_ref[...]))
    o_ref[...] = acc + _dot(p_ref[...].astype(BF16), wple_ref[...]) * gate


def _combine(r, meta, dest_flat, ys, p, w_ple, g_ple, w_gate):
    m, d = r.shape
    pd = p.shape[1]
    tm = 256
    row = lambda n: pl.BlockSpec((tm, n), lambda i: (i, 0))
    return pl.pallas_call(
        _combine_kernel,
        grid=(m // tm,),
        in_specs=[pl.BlockSpec((tm * TOP_K,), lambda i: (i,), memory_space=pltpu.SMEM),
                  row(d), row(LANES), row(pd), _full((pd, d)), _full((1, d)), _full((d, d)),
                  pl.BlockSpec(memory_space=pl.ANY)],
        out_specs=row(d),
        out_shape=jax.ShapeDtypeStruct((m, d), F32),
        scratch_shapes=[pltpu.VMEM((TOP_K, tm, d), F32), pltpu.SemaphoreType.DMA(())],
        compiler_params=_params("arbitrary"),
        name="moe_combine_ple",
    )(dest_flat, r, meta, p, w_ple.astype(BF16), g_ple.reshape(1, d), w_gate.astype(BF16), ys)


def _moe_ple(r, g_ffn, r_w, r_b, layer, w_gu, b_gu, w_dn, b_dn, p, w_ple, g_ple, w_gate):
    m, d = r.shape
    xn, meta, cnt = _router(r, g_ffn, r_w, r_b)
    top_i = meta[:, 0:TOP_K].astype(jnp.int32)
    rank = meta[:, 2 * TOP_K:3 * TOP_K].astype(jnp.int32)
    counts = cnt[0, :N_EXPERTS].astype(jnp.int32)
    padded = ((counts + EXPERT_ROWS - 1) // EXPERT_ROWS) * EXPERT_ROWS
    pad_end = jnp.cumsum(padded)
    pad_start = pad_end - padded
    dest = (pad_start[top_i] + rank).reshape(-1)
    n_rows = m * TOP_K + N_EXPERTS * EXPERT_ROWS
    n_blk = n_rows // EXPERT_ROWS
    blk_start = jnp.arange(n_blk, dtype=jnp.int32) * EXPERT_ROWS
    blk_e = jnp.sum((pad_end[None, :] <= blk_start[:, None]).astype(jnp.int32), axis=1)
    blk_e = jnp.minimum(blk_e, N_EXPERTS - 1)
    blk_nv = jnp.clip(counts[blk_e] - (blk_start - pad_start[blk_e]), 0, EXPERT_ROWS).astype(jnp.int32)
    unused = pad_end[-1] + jnp.arange(N_EXPERTS, dtype=jnp.int32) * EXPERT_ROWS
    tail_start = jnp.concatenate([jnp.where(padded > 0, pad_end - EXPERT_ROWS, -1),
                                  jnp.where(unused < n_rows, unused, -1)]).astype(jnp.int32)
    xs = _dispatch(xn, dest, tail_start, n_rows)
    ys = _experts(xs, blk_e, blk_nv, layer, w_gu, b_gu, w_dn, b_dn)
    return _combine(r, meta, dest, ys, p, w_ple, g_ple, w_gate)


def _seg_norm(x, bd, gain):
    w = x.shape[1]
    hi, lo = _split2(x * x)
    parts = []
    for c in range(w // MXU_DIM):
        cs = slice(c * MXU_DIM, (c + 1) * MXU_DIM)
        parts.append(_dot(hi[:, cs], bd) + _dot(lo[:, cs], bd))
    ss = parts[0] if len(parts) == 1 else jnp.concatenate(parts, axis=1)
    return x * lax.rsqrt(ss * (1.0 / ATT_HEAD_DIM) + NORM_EPS) * gain


def _rope64(x, cos, sin_signed):
    w = x.shape[1]
    half = ATT_HEAD_DIM // 2
    reps = w // LANES
    if reps > 1:
        cos = jnp.concatenate([cos] * reps, axis=1)
        sin_signed = jnp.concatenate([sin_signed] * reps, axis=1)
    lane = lax.broadcasted_iota(jnp.int32, x.shape, 1)
    first = (lane % ATT_HEAD_DIM) < half
    partner = jnp.where(first, pltpu.roll(x, w - half, axis=1), pltpu.roll(x, half, axis=1))
    return x * cos + partner * sin_signed


def _qkv_kernel(r_ref, gq_ref, gkv_ref, qw_ref, gw_ref, kvw_ref, qn_ref, kn_ref, cos_ref, sin_ref, bd_ref,
                qt_ref, gates_ref, kcv_ref, ks_ref, kw_ref, vst_ref, vwt_ref):
    xhat = _rms_hat(r_ref[...])
    hq = (xhat * gq_ref[...]).astype(BF16)
    hkv = (xhat * gkv_ref[...]).astype(BF16)
    cos = cos_ref[...]
    sin = sin_ref[...]
    bd = bd_ref[...]
    gw = ATT_KV_GROUPS * ATT_HEAD_DIM

    q = _dot(hq, qw_ref[...])
    q = _rope64(_seg_norm(q, bd, qn_ref[...]), cos, sin) * (ATT_HEAD_DIM ** -0.5 * LOG2E)
    qt_ref[0] = q.T.reshape(ATT_HEADS, ATT_HEAD_DIM, q.shape[0]).astype(BF16)
    gates_ref[...] = jax.nn.sigmoid(_dot(hq, gw_ref[...]))

    kv = _dot(hkv, kvw_ref[...])
    kcv_ref[...] = kv[:, 0:2 * gw]
    ks = _rope64(_seg_norm(kv[:, 2 * gw:3 * gw], bd, kn_ref[0:1, :]), cos, sin)
    kw = _rope64(_seg_norm(kv[:, 4 * gw:5 * gw], bd, kn_ref[1:2, :]), cos, sin)
    for g in range(ATT_KV_GROUPS):
        gs = slice(g * ATT_HEAD_DIM, (g + 1) * ATT_HEAD_DIM)
        ks_ref[0, g] = ks[:, gs].astype(BF16)
        kw_ref[0, g] = kw[:, gs].astype(BF16)
    vst = kv[:, 3 * gw:4 * gw].T.astype(BF16)
    vwt = kv[:, 5 * gw:6 * gw].T.astype(BF16)
    ones = jnp.ones((V_ROWS - ATT_HEAD_DIM, vst.shape[1]), BF16)
    for g in range(ATT_KV_GROUPS):
        gs = slice(g * ATT_HEAD_DIM, (g + 1) * ATT_HEAD_DIM)
        vst_ref[0, g, 0:ATT_HEAD_DIM, :] = vst[gs, :]
        vwt_ref[0, g, 0:ATT_HEAD_DIM, :] = vwt[gs, :]
        vst_ref[0, g, ATT_HEAD_DIM:V_ROWS, :] = ones
        vwt_ref[0, g, ATT_HEAD_DIM:V_ROWS, :] = ones


def _rope_tables(pos):
    dh = ATT_HEAD_DIM
    inv = 1.0 / (ROPE_THETA ** (jnp.arange(0, dh, 2, dtype=F32) / dh))
    ang = pos.astype(F32)[:, None] * inv
    c, s = jnp.cos(ang), jnp.sin(ang)
    cos = jnp.concatenate([c, c, c, c], axis=1)
    sin = jnp.concatenate([-s, s, -s, s], axis=1)
    return cos, sin


def _block_diag_ones():
    b = np.kron(np.eye(MXU_DIM // ATT_HEAD_DIM, dtype=np.float32),
                np.ones((ATT_HEAD_DIM, ATT_HEAD_DIM), np.float32))
    return jnp.asarray(b, dtype=BF16)


def _qkv(r, batch, g_q, g_kv, q_w, kv_w, q_norm, k_norm):
    m, d = r.shape
    seq = m // batch
    tm = 256
    nsb = seq // tm
    hd = ATT_HEADS * ATT_HEAD_DIM
    gw = ATT_KV_GROUPS * ATT_HEAD_DIM
    cos, sin = _rope_tables(jnp.arange(seq))
    qw = q_w[:, :hd].astype(BF16)
    gwt = jnp.pad(q_w[:, hd:], ((0, 0), (0, LANES - N_BRANCH * ATT_HEADS))).astype(BF16)
    qn = jnp.tile(q_norm, ATT_HEADS).reshape(1, hd)
    kn = jnp.stack([jnp.tile(k_norm[1], ATT_KV_GROUPS), jnp.tile(k_norm[2], ATT_KV_GROUPS)])
    row = lambda n: pl.BlockSpec((tm, n), lambda i: (i, 0))
    tab = pl.BlockSpec((tm, LANES), lambda i: (i % nsb, 0))
    vt = pl.BlockSpec((1, ATT_KV_GROUPS, V_ROWS, tm), lambda i: (i // nsb, 0, 0, i % nsb))
    kn_spec = pl.BlockSpec((1, ATT_KV_GROUPS, tm, ATT_HEAD_DIM), lambda i: (i // nsb, 0, i % nsb, 0))
    return pl.pallas_call(
        _qkv_kernel,
        grid=(m // tm,),
        in_specs=[row(d), _full((1, d)), _full((1, d)), _full((d, hd)), _full((d, LANES)),
                  _full((d, 6 * gw)), _full((1, hd)), _full((2, gw)), tab, tab,
                  _full((MXU_DIM, MXU_DIM))],
        out_specs=[pl.BlockSpec((1, ATT_HEADS, ATT_HEAD_DIM, tm), lambda i: (i // nsb, 0, 0, i % nsb)),
                   row(LANES), row(2 * gw), kn_spec, kn_spec, vt, vt],
        out_shape=[jax.ShapeDtypeStruct((batch, ATT_HEADS, ATT_HEAD_DIM, seq), BF16),
                   jax.ShapeDtypeStruct((m, LANES), F32),
                   jax.ShapeDtypeStruct((m, 2 * gw), F32),
                   jax.ShapeDtypeStruct((batch, ATT_KV_GROUPS, seq, ATT_HEAD_DIM), BF16),
                   jax.ShapeDtypeStruct((batch, ATT_KV_GROUPS, seq, ATT_HEAD_DIM), BF16),
                   jax.ShapeDtypeStruct((batch, ATT_KV_GROUPS, V_ROWS, seq), BF16),
                   jax.ShapeDtypeStruct((batch, ATT_KV_GROUPS, V_ROWS, seq), BF16)],
        compiler_params=_params("parallel"),
        name="nsa_qkv",
    )(r, g_q.reshape(1, d), g_kv.reshape(1, d), qw, gwt, kv_w.astype(BF16), qn, kn, cos, sin,
      _block_diag_ones())


def _cmp_kernel(x_ref, pe_ref, w1_ref, w2_ref, kn_ref, cos_ref, sin_ref, o_ref, ot_ref):
    nh = x_ref.shape[1]
    half_w = x_ref.shape[2]
    x = x_ref[0]
    h1 = _dot((x + pe_ref[0, 0:1, :]).astype(BF16), w1_ref[0, 0:half_w, :])
    h2 = _dot((x + pe_ref[0, 1:2, :]).astype(BF16), w1_ref[0, half_w:2 * half_w, :])
    pre = h1 + pltpu.roll(h2, nh - 1, axis=0)
    c = _dot(_silu(pre).astype(BF16), w2_ref[0])
    is_k = (pl.program_id(0) // ATT_KV_GROUPS) % 2 == 0
    cn = c * lax.rsqrt(jnp.sum(c * c, axis=-1, keepdims=True) * (1.0 / ATT_HEAD_DIM) + NORM_EPS) * kn_ref[...]
    cn = _rope64(cn, cos_ref[...], sin_ref[...])
    res = jnp.where(is_k, cn, c)
    rowi = lax.broadcasted_iota(jnp.int32, res.shape, 0)
    res = jnp.where(rowi < nh - 1, res, 0.0)
    o_ref[0] = res[:, 0:ATT_HEAD_DIM].astype(BF16)
    ot_ref[0] = res.T[0:ATT_HEAD_DIM, :].astype(BF16)


def _compress(kcv, batch, cmp_pe, cmp_w1, cmp_w2, k_norm0):
    m = kcv.shape[0]
    seq = m // batch
    nh = seq // CMP_STRIDE
    dh = ATT_HEAD_DIM
    half_w = CMP_STRIDE * dh
    hidden = cmp_w1.shape[2]
    x = kcv.reshape(batch, seq, 2, ATT_KV_GROUPS, dh).transpose(0, 2, 3, 1, 4)
    x = x.reshape(batch * 2 * ATT_KV_GROUPS, nh, half_w)
    pe = cmp_pe.reshape(2, 2, half_w)
    w2 = jnp.pad(cmp_w2, ((0, 0), (0, 0), (0, LANES - dh))).astype(BF16)
    kn = jnp.pad(k_norm0, (0, LANES - dh)).reshape(1, LANES)
    cos, sin = _rope_tables(jnp.arange(nh) * CMP_STRIDE + (CMP_BLOCK - 1))
    nidx = lambda i: (i // ATT_KV_GROUPS) % 2
    n_all = batch * 2 * ATT_KV_GROUPS
    c, ct = pl.pallas_call(
        _cmp_kernel,
        grid=(n_all,),
        in_specs=[pl.BlockSpec((1, nh, half_w), lambda i: (i, 0, 0)),
                  pl.BlockSpec((1, 2, half_w), lambda i: (nidx(i), 0, 0)),
                  pl.BlockSpec((1, 2 * half_w, hidden), lambda i: (nidx(i), 0, 0)),
                  pl.BlockSpec((1, hidden, LANES), lambda i: (nidx(i), 0, 0)),
                  _full((1, LANES)), _full((nh, LANES)), _full((nh, LANES))],
        out_specs=[pl.BlockSpec((1, nh, dh), lambda i: (i, 0, 0)),
                   pl.BlockSpec((1, dh, nh), lambda i: (i, 0, 0))],
        out_shape=[jax.ShapeDtypeStruct((n_all, nh, dh), BF16),
                   jax.ShapeDtypeStruct((n_all, dh, nh), BF16)],
        compiler_params=_params("parallel"),
        name="nsa_compress",
    )(x, pe, cmp_w1.astype(BF16), w2, kn, cos, sin)
    return (c.reshape(batch, 2, ATT_KV_GROUPS, nh, dh), ct.reshape(batch, 2, ATT_KV_GROUPS, dh, nh))


def _attn_kernel(qt_ref, gates_ref, kc_ref, vct_ref, ks_ref, vst_ref, kw_ref, vwt_ref, selwt_ref,
                 o_ref, s_ref, ps_ref, sel_ref, m_ref, acc_ref, oc_ref, ow_ref,
                 sa_ref, sb_ref, mxa_ref, mxb_ref, sw_ref):
    i = pl.program_id(2)
    n_cmp = kc_ref.shape[-2]
    n_sel = selwt_ref.shape[0]
    t0 = i * Q_BLOCK
    t_row = t0 + lax.broadcasted_iota(jnp.int32, (1, Q_BLOCK), 1)
    heads = range(ATT_REP)

    k_c = kc_ref[0, 0, 0]
    vt_c = vct_ref[0, 0, 0]
    cend = lax.broadcasted_iota(jnp.int32, (n_cmp, 1), 0) * CMP_STRIDE + (CMP_BLOCK - 1)
    ok_c = cend <= t_row
    q_all = jnp.concatenate([qt_ref[0, r] for r in heads], axis=1)
    s_c = _dot(k_c, q_all)
    for r in heads:
        s_ref[r, 0:n_cmp, :] = jnp.where(ok_c, s_c[:, r * Q_BLOCK:(r + 1) * Q_BLOCK], NEG)
    for r in heads:
        s = s_ref[r, 0:n_cmp, :]
        p = jnp.where(ok_c, jnp.exp2(s - jnp.max(s, axis=0, keepdims=True)), 0.0)
        den = jnp.sum(p, axis=0, keepdims=True)
        pc = p * (1.0 / jnp.where(den > 0, den, 1.0))
        oc_ref[r] = _dot(vt_c, pc.astype(BF16))
        if r == 0:
            ps_ref[...] = pc
        else:
            ps_ref[...] = ps_ref[...] + pc

    ps_hi, ps_mid, ps_lo = _split3(ps_ref[...])
    selwt = selwt_ref[...]
    imp = _dot(selwt, ps_hi) + _dot(selwt, ps_mid) + _dot(selwt, ps_lo)
    j = lax.broadcasted_iota(jnp.int32, (n_sel, Q_BLOCK), 0)
    jf = j.astype(F32)
    cur = t_row // SEL_BLOCK
    valid = j <= cur
    forced = (j == 0) | (j == cur) | (j == cur - 1)
    sc = jnp.where(forced, -jnp.inf, jnp.where(valid, imp, -1.0))
    sel = jnp.where(forced, 1.0, 0.0)
    for _ in range(min(SEL_TOPK, n_sel) - N_FORCED):
        mx = jnp.max(sc, axis=0, keepdims=True)
        ixf = jnp.min(jnp.where(sc == mx, jf, float(n_sel)), axis=0, keepdims=True)
        pick = jf == ixf
        sel = jnp.where(pick, 1.0, sel)
        sc = jnp.where(pick, -jnp.inf, sc)

    wk = WINDOW + Q_BLOCK
    st = pl.multiple_of(jnp.maximum(t0 - WINDOW, 0), LANES)
    k_w = kw_ref[0, 0, pl.ds(st, wk), :]
    vt_w = vwt_ref[0, 0, :, pl.ds(st, wk)]
    kpos_w = st + lax.broadcasted_iota(jnp.int32, (wk, 1), 0)
    ok_w = (kpos_w <= t_row) & (kpos_w > t_row - WINDOW)
    s_w = _dot(k_w, q_all)
    for r in heads:
        sw_ref[r] = jnp.where(ok_w, s_w[:, r * Q_BLOCK:(r + 1) * Q_BLOCK], NEG)
    for r0 in range(0, ATT_REP, 2):
        ps = []
        for r in (r0, r0 + 1):
            s = sw_ref[r]
            ps.append(jnp.exp2(s - jnp.max(s, axis=0, keepdims=True)).astype(BF16))
        res = _dot(vt_w, jnp.concatenate(ps, axis=1))
        for n, r in enumerate((r0, r0 + 1)):
            rs = res[:, n * Q_BLOCK:(n + 1) * Q_BLOCK]
            ow_ref[r] = rs[0:ATT_HEAD_DIM] / rs[ATT_HEAD_DIM:ATT_HEAD_DIM + 1]

    first_own = t0 // SEL_BLOCK
    sel_ref[...] = jnp.where(valid & (j < first_own), sel, 0.0)

    own = pl.multiple_of(t0, Q_BLOCK)
    vt_o = vst_ref[0, 0, :, pl.ds(own, Q_BLOCK)]
    s_o = _dot(ks_ref[0, 0, pl.ds(own, Q_BLOCK), :], q_all)
    tri = (lax.broadcasted_iota(jnp.int32, (Q_BLOCK, 1), 0)
           <= lax.broadcasted_iota(jnp.int32, (1, Q_BLOCK), 1))
    for r in heads:
        s = jnp.where(tri, s_o[:, r * Q_BLOCK:(r + 1) * Q_BLOCK], NEG)
        mx = jnp.max(s, axis=0, keepdims=True)
        m_ref[r:r + 1, :] = mx
        acc_ref[r] = _dot(vt_o, jnp.exp2(s - mx).astype(BF16))

    blocks = SEL_CHUNK // SEL_BLOCK
    last_chunk = ks_ref.shape[2] // SEL_CHUNK - 1

    def scores(c, buf_ref, mx_ref):
        cc = jnp.minimum(c, last_chunk)
        off = pl.multiple_of(cc * SEL_CHUNK, SEL_CHUNK)
        k_s = ks_ref[0, 0, pl.ds(off, SEL_CHUNK), :]
        picked = sel_ref[pl.ds(pl.multiple_of(cc * blocks, blocks), blocks), :]
        mk = jnp.broadcast_to(picked[:, None, :], (blocks, SEL_BLOCK, Q_BLOCK)).reshape(SEL_CHUNK, Q_BLOCK) > 0.5
        s_all = _dot(k_s, q_all)
        for r in heads:
            s = jnp.where(mk, s_all[:, r * Q_BLOCK:(r + 1) * Q_BLOCK], NEG)
            buf_ref[r] = s
            mx_ref[r:r + 1, :] = jnp.max(s, axis=0, keepdims=True)

    def accumulate(c, buf_ref, mx_ref):
        cc = jnp.minimum(c, last_chunk)
        off = pl.multiple_of(cc * SEL_CHUNK, SEL_CHUNK)
        vt_s = vst_ref[0, 0, :, pl.ds(off, SEL_CHUNK)]
        for r0 in range(0, ATT_REP, 2):
            ps, alphas = [], []
            for r in (r0, r0 + 1):
                m_old = m_ref[r:r + 1, :]
                m_new = jnp.maximum(m_old, mx_ref[r:r + 1, :])
                ps.append(jnp.exp2(buf_ref[r] - m_new).astype(BF16))
                alphas.append(jnp.exp2(m_old - m_new))
                m_ref[r:r + 1, :] = m_new
            res = _dot(vt_s, jnp.concatenate(ps, axis=1))
            for n, r in enumerate((r0, r0 + 1)):
                acc_ref[r] = alphas[n] * acc_ref[r] + res[:, n * Q_BLOCK:(n + 1) * Q_BLOCK]

    n_chunks = (first_own + blocks - 1) // blocks
    scores(0, sa_ref, mxa_ref)

    def body(k, carry):
        scores(2 * k + 1, sb_ref, mxb_ref)
        accumulate(2 * k, sa_ref, mxa_ref)
        scores(2 * k + 2, sa_ref, mxa_ref)
        accumulate(2 * k + 1, sb_ref, mxb_ref)
        return carry

    lax.fori_loop(0, n_chunks // 2, body, 0)

    @pl.when(n_chunks % 2 == 1)
    def _():
        accumulate(n_chunks - 1, sa_ref, mxa_ref)

    gts = gates_ref[0, 0, 0]
    outs = []
    for r in heads:
        g = [gts[r * N_BRANCH + b:r * N_BRANCH + b + 1, :] for b in range(N_BRANCH)]
        o_sel = acc_ref[r, 0:ATT_HEAD_DIM, :] / acc_ref[r, ATT_HEAD_DIM:ATT_HEAD_DIM + 1, :]
        outs.append(g[0] * oc_ref[r] + g[1] * o_sel + g[2] * ow_ref[r])
    for pr in range(ATT_REP // 2):
        pair = jnp.concatenate([outs[2 * pr], outs[2 * pr + 1]], axis=0)
        o_ref[0, :, pr * LANES:(pr + 1) * LANES] = pair.T


def _cmp_to_sel_weights(seq):
    n_cmp = (seq - CMP_BLOCK) // CMP_STRIDE + 1
    n_sel = seq // SEL_BLOCK
    cs = np.arange(n_cmp) * CMP_STRIDE
    ss = np.arange(n_sel) * SEL_BLOCK
    ov = np.minimum(cs[:, None] + CMP_BLOCK, ss[None, :] + SEL_BLOCK) - np.maximum(cs[:, None], ss[None, :])
    w = np.clip(ov, 0, None).astype(np.float32) / CMP_BLOCK
    w = np.concatenate([w, np.zeros((seq // CMP_STRIDE - n_cmp, n_sel), np.float32)], axis=0)
    return jnp.asarray(w.T, dtype=BF16)


def _attention(qt, gates, kc, kct, ks, kw, vst, vwt, batch):
    seq = qt.shape[3]
    nqb = seq // Q_BLOCK
    dh = ATT_HEAD_DIM
    n_cmp = seq // CMP_STRIDE
    n_sel = seq // SEL_BLOCK
    hd = ATT_HEADS * dh
    gw = ATT_KV_GROUPS * dh
    wk = WINDOW + Q_BLOCK
    g3 = gates[:, :N_BRANCH * ATT_HEADS].reshape(batch, nqb, Q_BLOCK, ATT_KV_GROUPS, ATT_REP * N_BRANCH)
    g3 = g3.transpose(0, 3, 1, 4, 2)
    g3 = jnp.pad(g3, ((0, 0), (0, 0), (0, 0), (0, GATE_ROWS - ATT_REP * N_BRANCH), (0, 0)))
    per_bg = lambda *shape: pl.BlockSpec((1, 1) + shape, lambda b, g, i: (b, g, 0, 0))
    head_tile = pltpu.VMEM((ATT_REP, dh, Q_BLOCK), F32)
    return pl.pallas_call(
        _attn_kernel,
        grid=(batch, ATT_KV_GROUPS, nqb),
        in_specs=[pl.BlockSpec((1, ATT_REP, dh, Q_BLOCK), lambda b, g, i: (b, g, 0, i)),
                  pl.BlockSpec((1, 1, 1, GATE_ROWS, Q_BLOCK), lambda b, g, i: (b, g, i, 0, 0)),
                  pl.BlockSpec((1, 1, 1, n_cmp, dh), lambda b, g, i: (b, 0, g, 0, 0)),
                  pl.BlockSpec((1, 1, 1, dh, n_cmp), lambda b, g, i: (b, 1, g, 0, 0)),
                  per_bg(seq, dh), per_bg(V_ROWS, seq), per_bg(seq, dh), per_bg(V_ROWS, seq),
                  pl.BlockSpec((n_sel, n_cmp), lambda b, g, i: (0, 0))],
        out_specs=pl.BlockSpec((1, Q_BLOCK, gw), lambda b, g, i: (b, i, g)),
        out_shape=jax.ShapeDtypeStruct((batch, seq, hd), F32),
        scratch_shapes=[pltpu.VMEM((ATT_REP, n_cmp, Q_BLOCK), F32),
                        pltpu.VMEM((n_cmp, Q_BLOCK), F32), pltpu.VMEM((n_sel, Q_BLOCK), F32),
                        pltpu.VMEM((SUBLANES, Q_BLOCK), F32),
                        pltpu.VMEM((ATT_REP, V_ROWS, Q_BLOCK), F32), head_tile, head_tile,
                        pltpu.VMEM((ATT_REP, SEL_CHUNK, Q_BLOCK), F32), pltpu.VMEM((ATT_REP, SEL_CHUNK, Q_BLOCK), F32),
                        pltpu.VMEM((SUBLANES, Q_BLOCK), F32), pltpu.VMEM((SUBLANES, Q_BLOCK), F32),
                        pltpu.VMEM((ATT_REP, wk, Q_BLOCK), F32)],
        compiler_params=_params("parallel", "parallel", "arbitrary"),
        name="nsa_attention",
    )(qt, g3, kc, kct, ks, vst, kw, vwt, _cmp_to_sel_weights(seq))


def _oproj_kernel(a_ref, r_ref, w_ref, o_ref):
    o_ref[...] = r_ref[...] + _dot(a_ref[...].astype(BF16), w_ref[...])


def _oproj(a, r, w):
    m, d = r.shape
    k = a.shape[1]
    tm = 512
    row = lambda n: pl.BlockSpec((tm, n), lambda i: (i, 0))
    return pl.pallas_call(
        _oproj_kernel,
        grid=(m // tm,),
        in_specs=[row(k), row(d), _full((k, d))],
        out_specs=row(d),
        out_shape=jax.ShapeDtypeStruct((m, d), F32),
        compiler_params=_params("parallel"),
        name="nsa_oproj",
    )(a, r, w.astype(BF16))


def kernel(x, p, g_mix, g_ffn, m_w_in, m_conv_w, m_conv_b, m_dt_bias, m_a_log, m_d, m_g_norm, m_w_out,
           kv_g, kv_w, cmp_pe, cmp_w1, cmp_w2, k_norm, q_w, q_norm, o_w,
           r_w, r_b, e_w_gu, e_b_gu, e_w_dn, e_b_dn, ple_w, ple_g, ple_gate_w):
    batch, seq, d = x.shape
    m = batch * seq
    depth = p.shape[0]
    n_a = m_w_in.shape[0]
    r = x.reshape(m, d)
    pf = p.reshape(depth, m, p.shape[-1])
    shared = None
    for i in range(depth):
        if i < n_a:
            z, xbc, dt = _inproj(r, g_mix[i], m_w_in[i])
            r = _ssd(z, xbc, dt, r, batch, m_conv_w[i], m_conv_b[i], m_dt_bias[i], m_a_log[i], m_d[i],
                     m_g_norm[i], m_w_out[i])
        else:
            jb = i - n_a
            qt, gates, kcv, ks, kw, vst, vwt = _qkv(r, batch, g_mix[i], kv_g, q_w[jb], kv_w, q_norm[jb], k_norm)
            if shared is None:
                shared = _compress(kcv, batch, cmp_pe, cmp_w1, cmp_w2, k_norm[0]) + (ks, kw, vst, vwt)
            a = _attention(qt, gates, *shared, batch)
            r = _oproj(a.reshape(m, -1), r, o_w[jb])
        r = _moe_ple(r, g_ffn[i], r_w[i], r_b[i], i, e_w_gu, e_b_gu[i], e_w_dn, e_b_dn[i],
                     pf[i], ple_w[i], ple_g[i], ple_gate_w[i])
    return r.reshape(batch, seq, d)
```

```python
import functools
import math

import numpy as np
import jax
import jax.numpy as jnp
from jax import lax
from jax.experimental import pallas as pl
from jax.experimental.pallas import tpu as pltpu

F32 = jnp.float32
BF16 = jnp.bfloat16

NORM_EPS = 1e-6
ROPE_THETA = 10000.0
SSM_HEADS = 32
SSM_HEAD_DIM = 64
SSM_GROUPS = 4
SSM_STATE = 128
SSM_CONV = 4
SSM_CHUNK = 256
SSM_D_INNER = SSM_HEADS * SSM_HEAD_DIM
SSM_GN = SSM_GROUPS * SSM_STATE
SSM_CONV_DIM = SSM_D_INNER + 2 * SSM_GN
SSM_GROUP_W = SSM_D_INNER // SSM_GROUPS
ATT_HEAD_DIM = 64
ATT_KV_GROUPS = 4
ATT_REP = 4
ATT_HEADS = ATT_KV_GROUPS * ATT_REP
N_BRANCH = 3
CMP_BLOCK = 32
CMP_STRIDE = 16
SEL_BLOCK = 64
SEL_TOPK = 16
WINDOW = 512
Q_BLOCK = 256
SEL_FORCE = 1000.0
N_FORCED = 3
SEL_CHUNK = 512
GATE_ROWS = 16
V_ROWS = 80
LOG2E = 1.4426950408889634
N_EXPERTS = 32
TOP_K = 4
SWIGLU_LIMIT = 7.0
SWIGLU_ALPHA = 1.702
EXPERT_ROWS = 256

LANES = 128
SUBLANES = 8
MXU_DIM = 256
VMEM_LIMIT_BYTES = 56 * 1024 * 1024
NEG = -1e30


def _params(*sem):
    return pltpu.CompilerParams(dimension_semantics=sem, vmem_limit_bytes=VMEM_LIMIT_BYTES)


def _dot(a, b):
    return jnp.dot(a, b, preferred_element_type=F32)


def _dot_nt(a, b):
    return lax.dot_general(a, b, (((1,), (1,)), ((), ())), preferred_element_type=F32)


def _split2(x):
    hi = x.astype(BF16)
    lo = (x - hi.astype(F32)).astype(BF16)
    return hi, lo


def _split3(x):
    hi = x.astype(BF16)
    r = x - hi.astype(F32)
    mid = r.astype(BF16)
    lo = (r - mid.astype(F32)).astype(BF16)
    return hi, mid, lo


def _dot_parts(parts, m):
    out = _dot(parts[0], m)
    for p in parts[1:]:
        out = out + _dot(p, m)
    return out


def _rms_hat(x):
    return x * lax.rsqrt(jnp.mean(x * x, axis=-1, keepdims=True) + NORM_EPS)


def _silu(x):
    return x * jax.nn.sigmoid(x)


def _full(shape):
    nd = len(shape)
    return pl.BlockSpec(shape, lambda *_: (0,) * nd)


def _inproj_kernel(x_ref, g_ref, wz_ref, wx_ref, wdh_ref, wdl_ref, z_ref, xbc_ref, dt_ref):
    h = _rms_hat(x_ref[...]) * g_ref[...]
    hb, hl = _split2(h)
    z_ref[...] = _dot(hb, wz_ref[...])
    xbc_ref[...] = _dot(hb, wx_ref[...])
    dt_ref[...] = _dot(hb, wdh_ref[...]) + _dot(hl, wdh_ref[...]) + _dot(hb, wdl_ref[...])


def _inproj(x, g, w_in):
    m, d = x.shape
    tm = 256
    wz = w_in[:, :SSM_D_INNER].astype(BF16)
    wx = w_in[:, SSM_D_INNER:SSM_D_INNER + SSM_CONV_DIM].astype(BF16)
    wd = jnp.pad(w_in[:, SSM_D_INNER + SSM_CONV_DIM:], ((0, 0), (0, LANES - SSM_HEADS)))
    wdh, wdl = _split2(wd)
    row = lambda n: pl.BlockSpec((tm, n), lambda i: (i, 0))
    return pl.pallas_call(
        _inproj_kernel,
        grid=(m // tm,),
        in_specs=[row(d), _full((1, d)), _full((d, SSM_D_INNER)), _full((d, SSM_CONV_DIM)),
                  _full((d, LANES)), _full((d, LANES))],
        out_specs=[row(SSM_D_INNER), row(SSM_CONV_DIM), row(LANES)],
        out_shape=[jax.ShapeDtypeStruct((m, SSM_D_INNER), F32),
                   jax.ShapeDtypeStruct((m, SSM_CONV_DIM), F32),
                   jax.ShapeDtypeStruct((m, LANES), F32)],
        compiler_params=_params("parallel"),
        name="ssm_inproj",
    )(x, g.reshape(1, d), wz, wx, wdh, wdl)


def _ssd_kernel(z_ref, xbc_ref, dt_ref, x_ref, cw_ref, cb_ref, dtb_ref, alog_ref, dsk_ref, gn_ref,
                wout_ref, exp_ref, o_ref, state_ref, ext_ref, y_ref):
    L = SSM_CHUNK
    c = pl.program_id(1)

    @pl.when(c == 0)
    def _():
        state_ref[...] = jnp.zeros_like(state_ref)
        ext_ref[0:SUBLANES, :] = jnp.zeros((SUBLANES, SSM_CONV_DIM), F32)

    ext_ref[SUBLANES:SUBLANES + L, :] = xbc_ref[...]
    acc = jnp.broadcast_to(cb_ref[...], (L, SSM_CONV_DIM))
    for k in range(SSM_CONV):
        off = SUBLANES - (SSM_CONV - 1) + k
        acc = acc + cw_ref[k:k + 1, :] * ext_ref[off:off + L, :]
    ext_ref[0:SUBLANES, :] = ext_ref[L:L + SUBLANES, :]
    xc = _silu(acc)
    xs = xc[:, :SSM_D_INNER]
    bm = xc[:, SSM_D_INNER:SSM_D_INNER + SSM_GN]
    cm = xc[:, SSM_D_INNER + SSM_GN:]

    dtr = dt_ref[...] + dtb_ref[...]
    dt = jnp.maximum(dtr, 0.0) + jnp.log(1.0 + jnp.exp(-jnp.abs(dtr)))
    a_neg = -jnp.exp(alog_ref[...])
    da = dt * a_neg
    row = lax.broadcasted_iota(jnp.int32, (L, L), 0)
    col = lax.broadcasted_iota(jnp.int32, (L, L), 1)
    causal = row >= col
    tri = jnp.where(causal, 1.0, 0.0).astype(BF16)
    da_hi, da_mid, da_lo = _split3(da)
    acs = _dot(tri, da_hi) + _dot(tri, da_mid) + _dot(tri, da_lo)
    a_last = acs[L - 1:L, :]
    d_end = jnp.exp(a_last - acs)
    d_start = jnp.exp(acs)
    cdec = jnp.broadcast_to(jnp.exp(a_last), (SUBLANES, LANES))

    ex = exp_ref[...]
    dt_e = _dot(dt.astype(BF16), ex)
    d_end_e = _dot(d_end.astype(BF16), ex)
    d_start_e = _dot_parts(_split2(d_start), ex)
    cdec_e = _dot_parts(_split2(cdec), ex)[0:1, :]

    xdt = xs * dt_e
    xb = xdt.astype(BF16)
    xd = (xdt * d_end_e).astype(BF16)
    acs_t = acs.T
    bm_t = bm.T
    bmb = bm.astype(BF16)
    cmb = cm.astype(BF16)

    for g in range(SSM_GROUPS):
        gs = slice(g * SSM_STATE, (g + 1) * SSM_STATE)
        ws = slice(g * SSM_GROUP_W, (g + 1) * SSM_GROUP_W)
        cb = _dot_nt(cmb[:, gs], bmb[:, gs])
        for r in range(SSM_HEADS // SSM_GROUPS):
            h = g * (SSM_HEADS // SSM_GROUPS) + r
            hs = slice(h * SSM_HEAD_DIM, (h + 1) * SSM_HEAD_DIM)
            diff = acs[:, h:h + 1] - acs_t[h:h + 1, :]
            lm = jnp.exp(jnp.where(causal, diff, -jnp.inf))
            y_ref[:, hs] = _dot((cb * lm).astype(BF16), xb[:, hs])
        st = state_ref[g]
        y_ref[:, ws] = y_ref[:, ws] + _dot(cmb[:, gs], st.astype(BF16)) * d_start_e[:, ws]
        state_ref[g] = cdec_e[:, ws] * st + _dot(bm_t[gs, :].astype(BF16), xd[:, ws])

    zz = z_ref[...]
    y = (y_ref[...] + dsk_ref[...] * xs) * _silu(zz)
    for g in range(SSM_GROUPS):
        ws = slice(g * SSM_GROUP_W, (g + 1) * SSM_GROUP_W)
        yg = y[:, ws]
        y_ref[:, ws] = yg * lax.rsqrt(jnp.mean(yg * yg, axis=-1, keepdims=True) + NORM_EPS)
    yn = (y_ref[...] * gn_ref[...]).astype(BF16)
    o_ref[...] = x_ref[...] + _dot(yn, wout_ref[...])


def _head_expand_matrix():
    e = np.zeros((LANES, SSM_D_INNER), np.float32)
    for h in range(SSM_HEADS):
        e[h, h * SSM_HEAD_DIM:(h + 1) * SSM_HEAD_DIM] = 1.0
    return jnp.asarray(e, dtype=BF16)


def _ssd(z, xbc, dt, x, batch, conv_w, conv_b, dt_bias, a_log, d_skip, g_norm, w_out):
    m, d = x.shape
    L = SSM_CHUNK
    nc = m // batch // L
    padh = lambda v: jnp.pad(v, (0, LANES - SSM_HEADS)).reshape(1, LANES)
    row = lambda n: pl.BlockSpec((L, n), lambda b, c: (b * nc + c, 0))
    return pl.pallas_call(
        _ssd_kernel,
        grid=(batch, nc),
        in_specs=[row(SSM_D_INNER), row(SSM_CONV_DIM), row(LANES), row(d),
                  _full((SSM_CONV, SSM_CONV_DIM)), _full((1, SSM_CONV_DIM)), _full((1, LANES)),
                  _full((1, LANES)), _full((1, SSM_D_INNER)), _full((1, SSM_D_INNER)),
                  _full((SSM_D_INNER, d)), _full((LANES, SSM_D_INNER))],
        out_specs=row(d),
        out_shape=jax.ShapeDtypeStruct((m, d), F32),
        scratch_shapes=[pltpu.VMEM((SSM_GROUPS, SSM_STATE, SSM_GROUP_W), F32),
                        pltpu.VMEM((L + 2 * SUBLANES, SSM_CONV_DIM), F32),
                        pltpu.VMEM((L, SSM_D_INNER), F32)],
        compiler_params=_params("parallel", "arbitrary"),
        name="ssm_scan",
    )(z, xbc, dt, x, conv_w.T, conv_b.reshape(1, -1), padh(dt_bias), padh(a_log),
      jnp.repeat(d_skip, SSM_HEAD_DIM).reshape(1, -1), g_norm.reshape(1, -1), w_out.astype(BF16),
      _head_expand_matrix())


def _router_kernel(r_ref, g_ref, rwh_ref, rwl_ref, rb_ref, xn_ref, meta_ref, cnt_ref, carry_ref):
    tm = r_ref.shape[0]
    i = pl.program_id(0)

    @pl.when(i == 0)
    def _():
        carry_ref[...] = jnp.zeros_like(carry_ref)

    xn = _rms_hat(r_ref[...]) * g_ref[...]
    xn_ref[...] = xn
    hb, hl = _split2(xn)
    logits = _dot(hb, rwh_ref[...]) + _dot(hl, rwh_ref[...]) + _dot(hb, rwl_ref[...]) + rb_ref[...]
    lane = lax.broadcasted_iota(jnp.int32, (tm, LANES), 1)
    vals, idxs = [], []
    cur = logits
    for _ in range(TOP_K):
        mx = jnp.max(cur, axis=-1, keepdims=True)
        ix = jnp.min(jnp.where(cur == mx, lane, LANES), axis=-1, keepdims=True)
        vals.append(mx)
        idxs.append(ix)
        cur = jnp.where(lane == ix, -jnp.inf, cur)
    es = [jnp.exp(v - vals[0]) for v in vals]
    den = es[0] + es[1] + es[2] + es[3]
    sel = jnp.zeros((tm, LANES), F32)
    for ix in idxs:
        sel = sel + jnp.where(lane == ix, 1.0, 0.0)
    row = lax.broadcasted_iota(jnp.int32, (tm, tm), 0)
    col = lax.broadcasted_iota(jnp.int32, (tm, tm), 1)
    strict = jnp.where(row > col, 1.0, 0.0).astype(BF16)
    before = _dot(strict, sel.astype(BF16)) + carry_ref[0:1, :]
    meta = jnp.zeros((tm, LANES), F32)
    for k in range(TOP_K):
        rank = jnp.sum(jnp.where(lane == idxs[k], before, 0.0), axis=-1, keepdims=True)
        meta = jnp.where(lane == k, idxs[k].astype(F32), meta)
        meta = jnp.where(lane == TOP_K + k, es[k] / den, meta)
        meta = jnp.where(lane == 2 * TOP_K + k, rank, meta)
    meta_ref[...] = meta
    carry = carry_ref[...] + jnp.sum(sel, axis=0, keepdims=True)
    carry_ref[...] = carry
    cnt_ref[...] = carry


def _router(r, g, r_w, r_b):
    m, d = r.shape
    tm = 512
    rw = jnp.pad(r_w, ((0, 0), (0, LANES - N_EXPERTS)))
    rwh, rwl = _split2(rw)
    rb = jnp.pad(r_b, (0, LANES - N_EXPERTS), constant_values=NEG).reshape(1, LANES)
    row = lambda n: pl.BlockSpec((tm, n), lambda i: (i, 0))
    return pl.pallas_call(
        _router_kernel,
        grid=(m // tm,),
        in_specs=[row(d), _full((1, d)), _full((d, LANES)), _full((d, LANES)), _full((1, LANES))],
        out_specs=[row(d), row(LANES), _full((SUBLANES, LANES))],
        out_shape=[jax.ShapeDtypeStruct((m, d), F32), jax.ShapeDtypeStruct((m, LANES), F32),
                   jax.ShapeDtypeStruct((SUBLANES, LANES), F32)],
        scratch_shapes=[pltpu.VMEM((SUBLANES, LANES), F32)],
        compiler_params=_params("arbitrary"),
        name="moe_router",
    )(r, g.reshape(1, d), rwh, rwl, rb)


def _for_each_row_copy(tm, row_copy):
    def start(tb, carry):
        base = pl.multiple_of(tb * SUBLANES, SUBLANES)
        for j in range(SUBLANES):
            for k in range(TOP_K):
                row_copy(base + j, k).start(priority=k % 2)
        return carry

    def wait(tb, carry):
        base = pl.multiple_of(tb * SUBLANES, SUBLANES)
        for j in range(SUBLANES):
            for k in range(TOP_K):
                row_copy(base + j, k).wait()
        return carry

    lax.fori_loop(0, tm // SUBLANES, start, 0)
    lax.fori_loop(0, tm // SUBLANES, wait, 0)


def _dispatch_kernel(tail_ref, dest_ref, xn_ref, xs_ref, zeros_ref, sem, zsem):
    tm = xn_ref.shape[0]
    rows = zeros_ref.shape[0]

    def tail_copy(e):
        start = pl.multiple_of(jnp.maximum(tail_ref[e], 0), rows)
        return pltpu.make_async_copy(zeros_ref, xs_ref.at[pl.ds(start, rows), :], zsem)

    @pl.when(pl.program_id(0) == 0)
    def _():
        zeros_ref[...] = jnp.zeros_like(zeros_ref)
        for e in range(2 * N_EXPERTS):
            pl.when(tail_ref[e] >= 0)(tail_copy(e).start)
        for e in range(2 * N_EXPERTS):
            pl.when(tail_ref[e] >= 0)(tail_copy(e).wait)

    def row_copy(t, k):
        d = dest_ref[t * TOP_K + k]
        return pltpu.make_async_copy(xn_ref.at[pl.ds(t, 1), :], xs_ref.at[pl.ds(d, 1), :], sem)

    _for_each_row_copy(tm, row_copy)


def _dispatch(xn, dest_flat, tail_start, n_rows):
    m, d = xn.shape
    tm = 256
    grid_spec = pltpu.PrefetchScalarGridSpec(
        num_scalar_prefetch=1,
        grid=(m // tm,),
        in_specs=[pl.BlockSpec((tm * TOP_K,), lambda i, tail: (i,), memory_space=pltpu.SMEM),
                  pl.BlockSpec((tm, d), lambda i, tail: (i, 0))],
        out_specs=pl.BlockSpec(memory_space=pl.ANY),
        scratch_shapes=[pltpu.VMEM((EXPERT_ROWS, d), F32), pltpu.SemaphoreType.DMA(()),
                        pltpu.SemaphoreType.DMA(())],
    )
    return pl.pallas_call(
        _dispatch_kernel,
        grid_spec=grid_spec,
        out_shape=jax.ShapeDtypeStruct((n_rows, d), F32),
        compiler_params=_params("arbitrary"),
        name="moe_dispatch",
    )(tail_start, dest_flat, xn)


def _expert_kernel(be_ref, nv_ref, xs_ref, wgu_ref, bgu_ref, wdn_ref, bdn_ref, perm_ref, ys_ref,
                   wgu_s, wdn_s):
    b = pl.program_id(0)
    e = be_ref[b]
    prev = be_ref[jnp.maximum(b - 1, 0)]
    d_ff2 = wgu_s.shape[1]

    @pl.when(jnp.logical_or(b == 0, e != prev))
    def _():
        for c in range(d_ff2 // MXU_DIM):
            cs = slice(c * MXU_DIM, (c + 1) * MXU_DIM)
            wgu_s[:, cs] = _dot(wgu_ref[0, :, cs].astype(BF16), perm_ref[...]).astype(BF16)
        wdn_s[...] = wdn_ref[0].astype(BF16)

    @pl.when(nv_ref[b] > 0)
    def _():
        h = _dot(xs_ref[...].astype(BF16), wgu_s[...]) + bgu_ref[0]
        acts = []
        for c in range(d_ff2 // MXU_DIM):
            gate = jnp.minimum(h[:, c * MXU_DIM:c * MXU_DIM + LANES], SWIGLU_LIMIT)
            up = jnp.clip(h[:, c * MXU_DIM + LANES:(c + 1) * MXU_DIM], -SWIGLU_LIMIT, SWIGLU_LIMIT)
            acts.append((up + 1.0) * (gate * jax.nn.sigmoid(SWIGLU_ALPHA * gate)))
        a = jnp.concatenate(acts, axis=1).astype(BF16)
        ys_ref[...] = _dot(a, wdn_s[...]) + bdn_ref[0]

    @pl.when(nv_ref[b] == 0)
    def _():
        ys_ref[...] = jnp.zeros_like(ys_ref)


def _deinterleave_matrix():
    p = np.zeros((MXU_DIM, MXU_DIM), np.float32)
    for j in range(LANES):
        p[2 * j, j] = 1.0
        p[2 * j + 1, LANES + j] = 1.0
    return jnp.asarray(p, dtype=BF16)


def _experts(xs, blk_e, blk_nv, layer, w_gu_all, b_gu, w_dn_all, b_dn):
    n_rows, d = xs.shape
    _, n_e, _, d_ff2 = w_gu_all.shape
    d_ff = d_ff2 // 2
    tm = EXPERT_ROWS
    nt = d_ff2 // MXU_DIM
    w_gu = w_gu_all.reshape(-1, d, d_ff2)
    w_dn = w_dn_all.reshape(-1, d_ff, d)
    base = layer * n_e
    bgu_p = b_gu.reshape(n_e, nt, LANES, 2).transpose(0, 1, 3, 2).reshape(n_e, 1, d_ff2)
    grid_spec = pltpu.PrefetchScalarGridSpec(
        num_scalar_prefetch=2,
        grid=(n_rows // tm,),
        in_specs=[pl.BlockSpec((tm, d), lambda b, be, nv: (b, 0)),
                  pl.BlockSpec((1, d, d_ff2), lambda b, be, nv: (base + be[b], 0, 0)),
                  pl.BlockSpec((1, 1, d_ff2), lambda b, be, nv: (be[b], 0, 0)),
                  pl.BlockSpec((1, d_ff, d), lambda b, be, nv: (base + be[b], 0, 0)),
                  pl.BlockSpec((1, 1, d), lambda b, be, nv: (be[b], 0, 0)),
                  pl.BlockSpec((MXU_DIM, MXU_DIM), lambda b, be, nv: (0, 0))],
        out_specs=pl.BlockSpec((tm, d), lambda b, be, nv: (b, 0)),
        scratch_shapes=[pltpu.VMEM((d, d_ff2), BF16), pltpu.VMEM((d_ff, d), BF16)],
    )
    return pl.pallas_call(
        _expert_kernel,
        grid_spec=grid_spec,
        out_shape=jax.ShapeDtypeStruct((n_rows, d), F32),
        compiler_params=_params("arbitrary"),
        name="moe_experts",
    )(blk_e, blk_nv, xs, w_gu, bgu_p, w_dn, b_dn.reshape(n_e, 1, d), _deinterleave_matrix())


def _combine_kernel(dest_ref, r_ref, meta_ref, p_ref, wple_ref, gple_ref, wgate_ref, ys_ref, o_ref,
                    buf, sem):
    tm = r_ref.shape[0]

    def row_copy(t, k):
        d = dest_ref[t * TOP_K + k]
        return pltpu.make_async_copy(ys_ref.at[pl.ds(d, 1), :], buf.at[k, pl.ds(t, 1), :], sem)

    _for_each_row_copy(tm, row_copy)

    acc = r_ref[...]
    meta = meta_ref[...]
    for k in range(TOP_K):
        acc = acc + meta[:, TOP_K + k:TOP_K + k + 1] * buf[k]
    gate = jax.nn.sigmoid(_dot((_rms_hat(acc) * gple_ref[...]).astype(BF16), wgate_ref[...]))
    o_ref[...] = acc + _dot(p_ref[...].astype(BF16), wple_ref[...]) * gate


def _combine(r, meta, dest_flat, ys, p, w_ple, g_ple, w_gate):
    m, d = r.shape
    pd = p.shape[1]
    tm = 256
    row = lambda n: pl.BlockSpec((tm, n), lambda i: (i, 0))
    return pl.pallas_call(
        _combine_kernel,
        grid=(m // tm,),
        in_specs=[pl.BlockSpec((tm * TOP_K,), lambda i: (i,), memory_space=pltpu.SMEM),
                  row(d), row(LANES), row(pd), _full((pd, d)), _full((1, d)), _full((d, d)),
                  pl.BlockSpec(memory_space=pl.ANY)],
        out_specs=row(d),
        out_shape=jax.ShapeDtypeStruct((m, d), F32),
        scratch_shapes=[pltpu.VMEM((TOP_K, tm, d), F32), pltpu.SemaphoreType.DMA(())],
        compiler_params=_params("arbitrary"),
        name="moe_combine_ple",
    )(dest_flat, r, meta, p, w_ple.astype(BF16), g_ple.reshape(1, d), w_gate.astype(BF16), ys)


def _moe_ple(r, g_ffn, r_w, r_b, layer, w_gu, b_gu, w_dn, b_dn, p, w_ple, g_ple, w_gate):
    m, d = r.shape
    xn, meta, cnt = _router(r, g_ffn, r_w, r_b)
    top_i = meta[:, 0:TOP_K].astype(jnp.int32)
    rank = meta[:, 2 * TOP_K:3 * TOP_K].astype(jnp.int32)
    counts = cnt[0, :N_EXPERTS].astype(jnp.int32)
    padded = ((counts + EXPERT_ROWS - 1) // EXPERT_ROWS) * EXPERT_ROWS
    pad_end = jnp.cumsum(padded)
    pad_start = pad_end - padded
    dest = (pad_start[top_i] + rank).reshape(-1)
    n_rows = m * TOP_K + N_EXPERTS * EXPERT_ROWS
    n_blk = n_rows // EXPERT_ROWS
    blk_start = jnp.arange(n_blk, dtype=jnp.int32) * EXPERT_ROWS
    blk_e = jnp.sum((pad_end[None, :] <= blk_start[:, None]).astype(jnp.int32), axis=1)
    blk_e = jnp.minimum(blk_e, N_EXPERTS - 1)
    blk_nv = jnp.clip(counts[blk_e] - (blk_start - pad_start[blk_e]), 0, EXPERT_ROWS).astype(jnp.int32)
    unused = pad_end[-1] + jnp.arange(N_EXPERTS, dtype=jnp.int32) * EXPERT_ROWS
    tail_start = jnp.concatenate([jnp.where(padded > 0, pad_end - EXPERT_ROWS, -1),
                                  jnp.where(unused < n_rows, unused, -1)]).astype(jnp.int32)
    xs = _dispatch(xn, dest, tail_start, n_rows)
    ys = _experts(xs, blk_e, blk_nv, layer, w_gu, b_gu, w_dn, b_dn)
    return _combine(r, meta, dest, ys, p, w_ple, g_ple, w_gate)


def _seg_norm(x, bd, gain):
    w = x.shape[1]
    hi, lo = _split2(x * x)
    parts = []
    for c in range(w // MXU_DIM):
        cs = slice(c * MXU_DIM, (c + 1) * MXU_DIM)
        parts.append(_dot(hi[:, cs], bd) + _dot(lo[:, cs], bd))
    ss = parts[0] if len(parts) == 1 else jnp.concatenate(parts, axis=1)
    return x * lax.rsqrt(ss * (1.0 / ATT_HEAD_DIM) + NORM_EPS) * gain


def _rope64(x, cos, sin_signed):
    w = x.shape[1]
    half = ATT_HEAD_DIM // 2
    reps = w // LANES
    if reps > 1:
        cos = jnp.concatenate([cos] * reps, axis=1)
        sin_signed = jnp.concatenate([sin_signed] * reps, axis=1)
    lane = lax.broadcasted_iota(jnp.int32, x.shape, 1)
    first = (lane % ATT_HEAD_DIM) < half
    partner = jnp.where(first, pltpu.roll(x, w - half, axis=1), pltpu.roll(x, half, axis=1))
    return x * cos + partner * sin_signed


def _qkv_kernel(r_ref, gq_ref, gkv_ref, qw_ref, gw_ref, kvw_ref, qn_ref, kn_ref, cos_ref, sin_ref, bd_ref,
                qt_ref, gates_ref, kcv_ref, ks_ref, kw_ref, vst_ref, vwt_ref):
    xhat = _rms_hat(r_ref[...])
    hq = (xhat * gq_ref[...]).astype(BF16)
    hkv = (xhat * gkv_ref[...]).astype(BF16)
    cos = cos_ref[...]
    sin = sin_ref[...]
    bd = bd_ref[...]
    gw = ATT_KV_GROUPS * ATT_HEAD_DIM

    q = _dot(hq, qw_ref[...])
    q = _rope64(_seg_norm(q, bd, qn_ref[...]), cos, sin) * (ATT_HEAD_DIM ** -0.5 * LOG2E)
    qt_ref[0] = q.T.reshape(ATT_HEADS, ATT_HEAD_DIM, q.shape[0]).astype(BF16)
    gates_ref[...] = jax.nn.sigmoid(_dot(hq, gw_ref[...]))

    kv = _dot(hkv, kvw_ref[...])
    kcv_ref[...] = kv[:, 0:2 * gw]
    ks = _rope64(_seg_norm(kv[:, 2 * gw:3 * gw], bd, kn_ref[0:1, :]), cos, sin)
    kw = _rope64(_seg_norm(kv[:, 4 * gw:5 * gw], bd, kn_ref[1:2, :]), cos, sin)
    for g in range(ATT_KV_GROUPS):
        gs = slice(g * ATT_HEAD_DIM, (g + 1) * ATT_HEAD_DIM)
        ks_ref[0, g] = ks[:, gs].astype(BF16)
        kw_ref[0, g] = kw[:, gs].astype(BF16)
    vst = kv[:, 3 * gw:4 * gw].T.astype(BF16)
    vwt = kv[:, 5 * gw:6 * gw].T.astype(BF16)
    ones = jnp.ones((V_ROWS - ATT_HEAD_DIM, vst.shape[1]), BF16)
    for g in range(ATT_KV_GROUPS):
        gs = slice(g * ATT_HEAD_DIM, (g + 1) * ATT_HEAD_DIM)
        vst_ref[0, g, 0:ATT_HEAD_DIM, :] = vst[gs, :]
        vwt_ref[0, g, 0:ATT_HEAD_DIM, :] = vwt[gs, :]
        vst_ref[0, g, ATT_HEAD_DIM:V_ROWS, :] = ones
        vwt_ref[0, g, ATT_HEAD_DIM:V_ROWS, :] = ones


def _rope_tables(pos):
    dh = ATT_HEAD_DIM
    inv = 1.0 / (ROPE_THETA ** (jnp.arange(0, dh, 2, dtype=F32) / dh))
    ang = pos.astype(F32)[:, None] * inv
    c, s = jnp.cos(ang), jnp.sin(ang)
    cos = jnp.concatenate([c, c, c, c], axis=1)
    sin = jnp.concatenate([-s, s, -s, s], axis=1)
    return cos, sin


def _block_diag_ones():
    b = np.kron(np.eye(MXU_DIM // ATT_HEAD_DIM, dtype=np.float32),
                np.ones((ATT_HEAD_DIM, ATT_HEAD_DIM), np.float32))
    return jnp.asarray(b, dtype=BF16)


def _qkv(r, batch, g_q, g_kv, q_w, kv_w, q_norm, k_norm):
    m, d = r.shape
    seq = m // batch
    tm = 256
    nsb = seq // tm
    hd = ATT_HEADS * ATT_HEAD_DIM
    gw = ATT_KV_GROUPS * ATT_HEAD_DIM
    cos, sin = _rope_tables(jnp.arange(seq))
    qw = q_w[:, :hd].astype(BF16)
    gwt = jnp.pad(q_w[:, hd:], ((0, 0), (0, LANES - N_BRANCH * ATT_HEADS))).astype(BF16)
    qn = jnp.tile(q_norm, ATT_HEADS).reshape(1, hd)
    kn = jnp.stack([jnp.tile(k_norm[1], ATT_KV_GROUPS), jnp.tile(k_norm[2], ATT_KV_GROUPS)])
    row = lambda n: pl.BlockSpec((tm, n), lambda i: (i, 0))
    tab = pl.BlockSpec((tm, LANES), lambda i: (i % nsb, 0))
    vt = pl.BlockSpec((1, ATT_KV_GROUPS, V_ROWS, tm), lambda i: (i // nsb, 0, 0, i % nsb))
    kn_spec = pl.BlockSpec((1, ATT_KV_GROUPS, tm, ATT_HEAD_DIM), lambda i: (i // nsb, 0, i % nsb, 0))
    return pl.pallas_call(
        _qkv_kernel,
        grid=(m // tm,),
        in_specs=[row(d), _full((1, d)), _full((1, d)), _full((d, hd)), _full((d, LANES)),
                  _full((d, 6 * gw)), _full((1, hd)), _full((2, gw)), tab, tab,
                  _full((MXU_DIM, MXU_DIM))],
        out_specs=[pl.BlockSpec((1, ATT_HEADS, ATT_HEAD_DIM, tm), lambda i: (i // nsb, 0, 0, i % nsb)),
                   row(LANES), row(2 * gw), kn_spec, kn_spec, vt, vt],
        out_shape=[jax.ShapeDtypeStruct((batch, ATT_HEADS, ATT_HEAD_DIM, seq), BF16),
                   jax.ShapeDtypeStruct((m, LANES), F32),
                   jax.ShapeDtypeStruct((m, 2 * gw), F32),
                   jax.ShapeDtypeStruct((batch, ATT_KV_GROUPS, seq, ATT_HEAD_DIM), BF16),
                   jax.ShapeDtypeStruct((batch, ATT_KV_GROUPS, seq, ATT_HEAD_DIM), BF16),
                   jax.ShapeDtypeStruct((batch, ATT_KV_GROUPS, V_ROWS, seq), BF16),
                   jax.ShapeDtypeStruct((batch, ATT_KV_GROUPS, V_ROWS, seq), BF16)],
        compiler_params=_params("parallel"),
        name="nsa_qkv",
    )(r, g_q.reshape(1, d), g_kv.reshape(1, d), qw, gwt, kv_w.astype(BF16), qn, kn, cos, sin,
      _block_diag_ones())


def _cmp_kernel(x_ref, pe_ref, w1_ref, w2_ref, kn_ref, cos_ref, sin_ref, o_ref, ot_ref):
    nh = x_ref.shape[1]
    half_w = x_ref.shape[2]
    x = x_ref[0]
    h1 = _dot((x + pe_ref[0, 0:1, :]).astype(BF16), w1_ref[0, 0:half_w, :])
    h2 = _dot((x + pe_ref[0, 1:2, :]).astype(BF16), w1_ref[0, half_w:2 * half_w, :])
    pre = h1 + pltpu.roll(h2, nh - 1, axis=0)
    c = _dot(_silu(pre).astype(BF16), w2_ref[0])
    is_k = (pl.program_id(0) // ATT_KV_GROUPS) % 2 == 0
    cn = c * lax.rsqrt(jnp.sum(c * c, axis=-1, keepdims=True) * (1.0 / ATT_HEAD_DIM) + NORM_EPS) * kn_ref[...]
    cn = _rope64(cn, cos_ref[...], sin_ref[...])
    res = jnp.where(is_k, cn, c)
    rowi = lax.broadcasted_iota(jnp.int32, res.shape, 0)
    res = jnp.where(rowi < nh - 1, res, 0.0)
    o_ref[0] = res[:, 0:ATT_HEAD_DIM].astype(BF16)
    ot_ref[0] = res.T[0:ATT_HEAD_DIM, :].astype(BF16)


def _compress(kcv, batch, cmp_pe, cmp_w1, cmp_w2, k_norm0):
    m = kcv.shape[0]
    seq = m // batch
    nh = seq // CMP_STRIDE
    dh = ATT_HEAD_DIM
    half_w = CMP_STRIDE * dh
    hidden = cmp_w1.shape[2]
    x = kcv.reshape(batch, seq, 2, ATT_KV_GROUPS, dh).transpose(0, 2, 3, 1, 4)
    x = x.reshape(batch * 2 * ATT_KV_GROUPS, nh, half_w)
    pe = cmp_pe.reshape(2, 2, half_w)
    w2 = jnp.pad(cmp_w2, ((0, 0), (0, 0), (0, LANES - dh))).astype(BF16)
    kn = jnp.pad(k_norm0, (0, LANES - dh)).reshape(1, LANES)
    cos, sin = _rope_tables(jnp.arange(nh) * CMP_STRIDE + (CMP_BLOCK - 1))
    nidx = lambda i: (i // ATT_KV_GROUPS) % 2
    n_all = batch * 2 * ATT_KV_GROUPS
    c, ct = pl.pallas_call(
        _cmp_kernel,
        grid=(n_all,),
        in_specs=[pl.BlockSpec((1, nh, half_w), lambda i: (i, 0, 0)),
                  pl.BlockSpec((1, 2, half_w), lambda i: (nidx(i), 0, 0)),
                  pl.BlockSpec((1, 2 * half_w, hidden), lambda i: (nidx(i), 0, 0)),
                  pl.BlockSpec((1, hidden, LANES), lambda i: (nidx(i), 0, 0)),
                  _full((1, LANES)), _full((nh, LANES)), _full((nh, LANES))],
        out_specs=[pl.BlockSpec((1, nh, dh), lambda i: (i, 0, 0)),
                   pl.BlockSpec((1, dh, nh), lambda i: (i, 0, 0))],
        out_shape=[jax.ShapeDtypeStruct((n_all, nh, dh), BF16),
                   jax.ShapeDtypeStruct((n_all, dh, nh), BF16)],
        compiler_params=_params("parallel"),
        name="nsa_compress",
    )(x, pe, cmp_w1.astype(BF16), w2, kn, cos, sin)
    return (c.reshape(batch, 2, ATT_KV_GROUPS, nh, dh), ct.reshape(batch, 2, ATT_KV_GROUPS, dh, nh))


def _attn_kernel(qt_ref, gates_ref, kc_ref, vct_ref, ks_ref, vst_ref, kw_ref, vwt_ref, selwt_ref,
                 o_ref, s_ref, ps_ref, sel_ref, m_ref, acc_ref, oc_ref, ow_ref,
                 sa_ref, sb_ref, mxa_ref, mxb_ref, sw_ref):
    i = pl.program_id(2)
    n_cmp = kc_ref.shape[-2]
    n_sel = selwt_ref.shape[0]
    t0 = i * Q_BLOCK
    t_row = t0 + lax.broadcasted_iota(jnp.int32, (1, Q_BLOCK), 1)
    heads = range(ATT_REP)

    k_c = kc_ref[0, 0, 0]
    vt_c = vct_ref[0, 0, 0]
    cend = lax.broadcasted_iota(jnp.int32, (n_cmp, 1), 0) * CMP_STRIDE + (CMP_BLOCK - 1)
    ok_c = cend <= t_row
    q_all = jnp.concatenate([qt_ref[0, r] for r in heads], axis=1)
    s_c = _dot(k_c, q_all)
    for r in heads:
        s_ref[r, 0:n_cmp, :] = jnp.where(ok_c, s_c[:, r * Q_BLOCK:(r + 1) * Q_BLOCK], NEG)
    for r in heads:
        s = s_ref[r, 0:n_cmp, :]
        p = jnp.where(ok_c, jnp.exp2(s - jnp.max(s, axis=0, keepdims=True)), 0.0)
        den = jnp.sum(p, axis=0, keepdims=True)
        pc = p * (1.0 / jnp.where(den > 0, den, 1.0))
        oc_ref[r] = _dot(vt_c, pc.astype(BF16))
        if r == 0:
            ps_ref[...] = pc
        else:
            ps_ref[...] = ps_ref[...] + pc

    ps_hi, ps_mid, ps_lo = _split3(ps_ref[...])
    selwt = selwt_ref[...]
    imp = _dot(selwt, ps_hi) + _dot(selwt, ps_mid) + _dot(selwt, ps_lo)
    j = lax.broadcasted_iota(jnp.int32, (n_sel, Q_BLOCK), 0)
    jf = j.astype(F32)
    cur = t_row // SEL_BLOCK
    valid = j <= cur
    forced = (j == 0) | (j == cur) | (j == cur - 1)
    sc = jnp.where(forced, -jnp.inf, jnp.where(valid, imp, -1.0))
    sel = jnp.where(forced, 1.0, 0.0)
    for _ in range(min(SEL_TOPK, n_sel) - N_FORCED):
        mx = jnp.max(sc, axis=0, keepdims=True)
        ixf = jnp.min(jnp.where(sc == mx, jf, float(n_sel)), axis=0, keepdims=True)
        pick = jf == ixf
        sel = jnp.where(pick, 1.0, sel)
        sc = jnp.where(pick, -jnp.inf, sc)

    wk = WINDOW + Q_BLOCK
    st = pl.multiple_of(jnp.maximum(t0 - WINDOW, 0), LANES)
    k_w = kw_ref[0, 0, pl.ds(st, wk), :]
    vt_w = vwt_ref[0, 0, :, pl.ds(st, wk)]
    kpos_w = st + lax.broadcasted_iota(jnp.int32, (wk, 1), 0)
    ok_w = (kpos_w <= t_row) & (kpos_w > t_row - WINDOW)
    s_w = _dot(k_w, q_all)
    for r in heads:
        sw_ref[r] = jnp.where(ok_w, s_w[:, r * Q_BLOCK:(r + 1) * Q_BLOCK], NEG)
    for r0 in range(0, ATT_REP, 2):
        ps = []
        for r in (r0, r0 + 1):
            s = sw_ref[r]
            ps.append(jnp.exp2(s - jnp.max(s, axis=0, keepdims=True)).astype(BF16))
        res = _dot(vt_w, jnp.concatenate(ps, axis=1))
        for n, r in enumerate((r0, r0 + 1)):
            rs = res[:, n * Q_BLOCK:(n + 1) * Q_BLOCK]
            ow_ref[r] = rs[0:ATT_HEAD_DIM] / rs[ATT_HEAD_DIM:ATT_HEAD_DIM + 1]

    sel_ref[...] = jnp.where(valid, sel, 0.0)

    blocks = SEL_CHUNK // SEL_BLOCK
    last_chunk = ks_ref.shape[2] // SEL_CHUNK - 1
    n_chunks = t0 // SEL_CHUNK
    off_d = pl.multiple_of(n_chunks * SEL_CHUNK, SEL_CHUNK)
    picked_d = sel_ref[pl.ds(pl.multiple_of(n_chunks * blocks, blocks), blocks), :]
    keep_d = jnp.broadcast_to(picked_d[:, None, :], (blocks, SEL_BLOCK, Q_BLOCK)).reshape(SEL_CHUNK, Q_BLOCK)
    kpos_d = off_d + lax.broadcasted_iota(jnp.int32, (SEL_CHUNK, 1), 0)
    mk_d = jnp.where(kpos_d <= t_row, keep_d, 0.0) > 0.5
    s_d = _dot(ks_ref[0, 0, pl.ds(off_d, SEL_CHUNK), :], q_all)
    vt_d = vst_ref[0, 0, :, pl.ds(off_d, SEL_CHUNK)]
    for r0 in range(0, ATT_REP, 2):
        ps = []
        for r in (r0, r0 + 1):
            s = jnp.where(mk_d, s_d[:, r * Q_BLOCK:(r + 1) * Q_BLOCK], NEG)
            mx = jnp.max(s, axis=0, keepdims=True)
            m_ref[r:r + 1, :] = mx
            ps.append(jnp.exp2(s - mx).astype(BF16))
        res = _dot(vt_d, jnp.concatenate(ps, axis=1))
        for n, r in enumerate((r0, r0 + 1)):
            acc_ref[r] = res[:, n * Q_BLOCK:(n + 1) * Q_BLOCK]


    def scores(c, buf_ref, mx_ref):
        cc = jnp.minimum(c, last_chunk)
        off = pl.multiple_of(cc * SEL_CHUNK, SEL_CHUNK)
        k_s = ks_ref[0, 0, pl.ds(off, SEL_CHUNK), :]
        picked = sel_ref[pl.ds(pl.multiple_of(cc * blocks, blocks), blocks), :]
        mk = jnp.broadcast_to(picked[:, None, :], (blocks, SEL_BLOCK, Q_BLOCK)).reshape(SEL_CHUNK, Q_BLOCK) > 0.5
        s_all = _dot(k_s, q_all)
        for r in heads:
            s = jnp.where(mk, s_all[:, r * Q_BLOCK:(r + 1) * Q_BLOCK], NEG)
            buf_ref[r] = s
            mx_ref[r:r + 1, :] = jnp.max(s, axis=0, keepdims=True)

    def accumulate(c, buf_ref, mx_ref):
        cc = jnp.minimum(c, last_chunk)
        off = pl.multiple_of(cc * SEL_CHUNK, SEL_CHUNK)
        vt_s = vst_ref[0, 0, :, pl.ds(off, SEL_CHUNK)]
        for r0 in range(0, ATT_REP, 2):
            ps, alphas = [], []
            for r in (r0, r0 + 1):
                m_old = m_ref[r:r + 1, :]
                m_new = jnp.maximum(m_old, mx_ref[r:r + 1, :])
                ps.append(jnp.exp2(buf_ref[r] - m_new).astype(BF16))
                alphas.append(jnp.exp2(m_old - m_new))
                m_ref[r:r + 1, :] = m_new
            res = _dot(vt_s, jnp.concatenate(ps, axis=1))
            for n, r in enumerate((r0, r0 + 1)):
                acc_ref[r] = alphas[n] * acc_ref[r] + res[:, n * Q_BLOCK:(n + 1) * Q_BLOCK]

    scores(0, sa_ref, mxa_ref)

    def body(k, carry):
        scores(2 * k + 1, sb_ref, mxb_ref)
        accumulate(2 * k, sa_ref, mxa_ref)
        scores(2 * k + 2, sa_ref, mxa_ref)
        accumulate(2 * k + 1, sb_ref, mxb_ref)
        return carry

    lax.fori_loop(0, n_chunks // 2, body, 0)

    @pl.when(n_chunks % 2 == 1)
    def _():
        accumulate(n_chunks - 1, sa_ref, mxa_ref)

    gts = gates_ref[0, 0, 0]
    outs = []
    for r in heads:
        g = [gts[r * N_BRANCH + b:r * N_BRANCH + b + 1, :] for b in range(N_BRANCH)]
        o_sel = acc_ref[r, 0:ATT_HEAD_DIM, :] / acc_ref[r, ATT_HEAD_DIM:ATT_HEAD_DIM + 1, :]
        outs.append(g[0] * oc_ref[r] + g[1] * o_sel + g[2] * ow_ref[r])
    for pr in range(ATT_REP // 2):
        pair = jnp.concatenate([outs[2 * pr], outs[2 * pr + 1]], axis=0)
        o_ref[0, :, pr * LANES:(pr + 1) * LANES] = pair.T


def _cmp_to_sel_weights(seq):
    n_cmp = (seq - CMP_BLOCK) // CMP_STRIDE + 1
    n_sel = seq // SEL_BLOCK
    cs = np.arange(n_cmp) * CMP_STRIDE
    ss = np.arange(n_sel) * SEL_BLOCK
    ov = np.minimum(cs[:, None] + CMP_BLOCK, ss[None, :] + SEL_BLOCK) - np.maximum(cs[:, None], ss[None, :])
    w = np.clip(ov, 0, None).astype(np.float32) / CMP_BLOCK
    w = np.concatenate([w, np.zeros((seq // CMP_STRIDE - n_cmp, n_sel), np.float32)], axis=0)
    return jnp.asarray(w.T, dtype=BF16)


def _attention(qt, gates, kc, kct, ks, kw, vst, vwt, batch):
    seq = qt.shape[3]
    nqb = seq // Q_BLOCK
    dh = ATT_HEAD_DIM
    n_cmp = seq // CMP_STRIDE
    n_sel = seq // SEL_BLOCK
    hd = ATT_HEADS * dh
    gw = ATT_KV_GROUPS * dh
    wk = WINDOW + Q_BLOCK
    g3 = gates[:, :N_BRANCH * ATT_HEADS].reshape(batch, nqb, Q_BLOCK, ATT_KV_GROUPS, ATT_REP * N_BRANCH)
    g3 = g3.transpose(0, 3, 1, 4, 2)
    g3 = jnp.pad(g3, ((0, 0), (0, 0), (0, 0), (0, GATE_ROWS - ATT_REP * N_BRANCH), (0, 0)))
    per_bg = lambda *shape: pl.BlockSpec((1, 1) + shape, lambda b, g, i: (b, g, 0, 0))
    head_tile = pltpu.VMEM((ATT_REP, dh, Q_BLOCK), F32)
    return pl.pallas_call(
        _attn_kernel,
        grid=(batch, ATT_KV_GROUPS, nqb),
        in_specs=[pl.BlockSpec((1, ATT_REP, dh, Q_BLOCK), lambda b, g, i: (b, g, 0, i)),
                  pl.BlockSpec((1, 1, 1, GATE_ROWS, Q_BLOCK), lambda b, g, i: (b, g, i, 0, 0)),
                  pl.BlockSpec((1, 1, 1, n_cmp, dh), lambda b, g, i: (b, 0, g, 0, 0)),
                  pl.BlockSpec((1, 1, 1, dh, n_cmp), lambda b, g, i: (b, 1, g, 0, 0)),
                  per_bg(seq, dh), per_bg(V_ROWS, seq), per_bg(seq, dh), per_bg(V_ROWS, seq),
                  pl.BlockSpec((n_sel, n_cmp), lambda b, g, i: (0, 0))],
        out_specs=pl.BlockSpec((1, Q_BLOCK, gw), lambda b, g, i: (b, i, g)),
        out_shape=jax.ShapeDtypeStruct((batch, seq, hd), F32),
        scratch_shapes=[pltpu.VMEM((ATT_REP, n_cmp, Q_BLOCK), F32),
                        pltpu.VMEM((n_cmp, Q_BLOCK), F32), pltpu.VMEM((n_sel, Q_BLOCK), F32),
                        pltpu.VMEM((SUBLANES, Q_BLOCK), F32),
                        pltpu.VMEM((ATT_REP, V_ROWS, Q_BLOCK), F32), head_tile, head_tile,
                        pltpu.VMEM((ATT_REP, SEL_CHUNK, Q_BLOCK), F32), pltpu.VMEM((ATT_REP, SEL_CHUNK, Q_BLOCK), F32),
                        pltpu.VMEM((SUBLANES, Q_BLOCK), F32), pltpu.VMEM((SUBLANES, Q_BLOCK), F32),
                        pltpu.VMEM((ATT_REP, wk, Q_BLOCK), F32)],
        compiler_params=_params("parallel", "parallel", "arbitrary"),
        name="nsa_attention",
    )(qt, g3, kc, kct, ks, vst, kw, vwt, _cmp_to_sel_weights(seq))


def _oproj_kernel(a_ref, r_ref, w_ref, o_ref):
    o_ref[...] = r_ref[...] + _dot(a_ref[...].astype(BF16), w_ref[...])


def _oproj(a, r, w):
    m, d = r.shape
    k = a.shape[1]
    tm = 512
    row = lambda n: pl.BlockSpec((tm, n), lambda i: (i, 0))
    return pl.pallas_call(
        _oproj_kernel,
        grid=(m // tm,),
        in_specs=[row(k), row(d), _full((k, d))],
        out_specs=row(d),
        out_shape=jax.ShapeDtypeStruct((m, d), F32),
        compiler_params=_params("parallel"),
        name="nsa_oproj",
    )(a, r, w.astype(BF16))


def kernel(x, p, g_mix, g_ffn, m_w_in, m_conv_w, m_conv_b, m_dt_bias, m_a_log, m_d, m_g_norm, m_w_out,
           kv_g, kv_w, cmp_pe, cmp_w1, cmp_w2, k_norm, q_w, q_norm, o_w,
           r_w, r_b, e_w_gu, e_b_gu, e_w_dn, e_b_dn, ple_w, ple_g, ple_gate_w):
    batch, seq, d = x.shape
    m = batch * seq
    depth = p.shape[0]
    n_a = m_w_in.shape[0]
    r = x.reshape(m, d)
    pf = p.reshape(depth, m, p.shape[-1])
    shared = None
    for i in range(depth):
        if i < n_a:
            z, xbc, dt = _inproj(r, g_mix[i], m_w_in[i])
            r = _ssd(z, xbc, dt, r, batch, m_conv_w[i], m_conv_b[i], m_dt_bias[i], m_a_log[i], m_d[i],
                     m_g_norm[i], m_w_out[i])
        else:
            jb = i - n_a
            qt, gates, kcv, ks, kw, vst, vwt = _qkv(r, batch, g_mix[i], kv_g, q_w[jb], kv_w, q_norm[jb], k_norm)
            if shared is None:
                shared = _compress(kcv, batch, cmp_pe, cmp_w1, cmp_w2, k_norm[0]) + (ks, kw, vst, vwt)
            a = _attention(qt, gates, *shared, batch)
            r = _oproj(a.reshape(m, -1), r, o_w[jb])
        r = _moe_ple(r, g_ffn[i], r_w[i], r_b[i], i, e_w_gu, e_b_gu[i], e_w_dn, e_b_dn[i],
                     pf[i], ple_w[i], ple_g[i], ple_gate_w[i])
    return r.reshape(batch, seq, d)
```

```python
import functools
import math

import numpy as np
import jax
import jax.numpy as jnp
from jax import lax
from jax.experimental import pallas as pl
from jax.experimental.pallas import tpu as pltpu

F32 = jnp.float32
BF16 = jnp.bfloat16

NORM_EPS = 1e-6
ROPE_THETA = 10000.0
SSM_HEADS = 32
SSM_HEAD_DIM = 64
SSM_GROUPS = 4
SSM_STATE = 128
SSM_CONV = 4
SSM_CHUNK = 256
SSM_D_INNER = SSM_HEADS * SSM_HEAD_DIM
SSM_GN = SSM_GROUPS * SSM_STATE
SSM_CONV_DIM = SSM_D_INNER + 2 * SSM_GN
SSM_GROUP_W = SSM_D_INNER // SSM_GROUPS
ATT_HEAD_DIM = 64
ATT_KV_GROUPS = 4
ATT_REP = 4
ATT_HEADS = ATT_KV_GROUPS * ATT_REP
N_BRANCH = 3
CMP_BLOCK = 32
CMP_STRIDE = 16
SEL_BLOCK = 64
SEL_TOPK = 16
WINDOW = 512
Q_BLOCK = 256
SEL_FORCE = 1000.0
N_FORCED = 3
SEL_CHUNK = 512
GATE_ROWS = 16
V_ROWS = 80
LOG2E = 1.4426950408889634
N_EXPERTS = 32
TOP_K = 4
SWIGLU_LIMIT = 7.0
SWIGLU_ALPHA = 1.702
EXPERT_ROWS = 256

LANES = 128
SUBLANES = 8
MXU_DIM = 256
VMEM_LIMIT_BYTES = 56 * 1024 * 1024
NEG = -1e30


def _params(*sem):
    return pltpu.CompilerParams(dimension_semantics=sem, vmem_limit_bytes=VMEM_LIMIT_BYTES)


def _dot(a, b):
    return jnp.dot(a, b, preferred_element_type=F32)


def _dot_nt(a, b):
    return lax.dot_general(a, b, (((1,), (1,)), ((), ())), preferred_element_type=F32)


def _split2(x):
    hi = x.astype(BF16)
    lo = (x - hi.astype(F32)).astype(BF16)
    return hi, lo


def _split3(x):
    hi = x.astype(BF16)
    r = x - hi.astype(F32)
    mid = r.astype(BF16)
    lo = (r - mid.astype(F32)).astype(BF16)
    return hi, mid, lo


def _dot_parts(parts, m):
    out = _dot(parts[0], m)
    for p in parts[1:]:
        out = out + _dot(p, m)
    return out


def _rms_hat(x):
    return x * lax.rsqrt(jnp.mean(x * x, axis=-1, keepdims=True) + NORM_EPS)


def _silu(x):
    return x * jax.nn.sigmoid(x)


def _full(shape):
    nd = len(shape)
    return pl.BlockSpec(shape, lambda *_: (0,) * nd)


def _inproj_kernel(x_ref, g_ref, wz_ref, wx_ref, wdh_ref, wdl_ref, z_ref, xbc_ref, dt_ref):
    h = _rms_hat(x_ref[...]) * g_ref[...]
    hb, hl = _split2(h)
    z_ref[...] = _dot(hb, wz_ref[...])
    xbc_ref[...] = _dot(hb, wx_ref[...])
    dt_ref[...] = _dot(hb, wdh_ref[...]) + _dot(hl, wdh_ref[...]) + _dot(hb, wdl_ref[...])


def _inproj(x, g, w_in):
    m, d = x.shape
    tm = 256
    wz = w_in[:, :SSM_D_INNER].astype(BF16)
    wx = w_in[:, SSM_D_INNER:SSM_D_INNER + SSM_CONV_DIM].astype(BF16)
    wd = jnp.pad(w_in[:, SSM_D_INNER + SSM_CONV_DIM:], ((0, 0), (0, LANES - SSM_HEADS)))
    wdh, wdl = _split2(wd)
    row = lambda n: pl.BlockSpec((tm, n), lambda i: (i, 0))
    return pl.pallas_call(
        _inproj_kernel,
        grid=(m // tm,),
        in_specs=[row(d), _full((1, d)), _full((d, SSM_D_INNER)), _full((d, SSM_CONV_DIM)),
                  _full((d, LANES)), _full((d, LANES))],
        out_specs=[row(SSM_D_INNER), row(SSM_CONV_DIM), row(LANES)],
        out_shape=[jax.ShapeDtypeStruct((m, SSM_D_INNER), F32),
                   jax.ShapeDtypeStruct((m, SSM_CONV_DIM), F32),
                   jax.ShapeDtypeStruct((m, LANES), F32)],
        compiler_params=_params("parallel"),
        name="ssm_inproj",
    )(x, g.reshape(1, d), wz, wx, wdh, wdl)


def _ssd_kernel(z_ref, xbc_ref, dt_ref, x_ref, cw_ref, cb_ref, dtb_ref, alog_ref, dsk_ref, gn_ref,
                wout_ref, exp_ref, o_ref, state_ref, ext_ref, y_ref):
    L = SSM_CHUNK
    c = pl.program_id(1)

    @pl.when(c == 0)
    def _():
        state_ref[...] = jnp.zeros_like(state_ref)
        ext_ref[0:SUBLANES, :] = jnp.zeros((SUBLANES, SSM_CONV_DIM), F32)

    ext_ref[SUBLANES:SUBLANES + L, :] = xbc_ref[...]
    acc = jnp.broadcast_to(cb_ref[...], (L, SSM_CONV_DIM))
    for k in range(SSM_CONV):
        off = SUBLANES - (SSM_CONV - 1) + k
        acc = acc + cw_ref[k:k + 1, :] * ext_ref[off:off + L, :]
    ext_ref[0:SUBLANES, :] = ext_ref[L:L + SUBLANES, :]
    xc = _silu(acc)
    xs = xc[:, :SSM_D_INNER]
    bm = xc[:, SSM_D_INNER:SSM_D_INNER + SSM_GN]
    cm = xc[:, SSM_D_INNER + SSM_GN:]

    dtr = dt_ref[...] + dtb_ref[...]
    dt = jnp.maximum(dtr, 0.0) + jnp.log(1.0 + jnp.exp(-jnp.abs(dtr)))
    a_neg = -jnp.exp(alog_ref[...])
    da = dt * a_neg
    row = lax.broadcasted_iota(jnp.int32, (L, L), 0)
    col = lax.broadcasted_iota(jnp.int32, (L, L), 1)
    causal = row >= col
    tri = jnp.where(causal, 1.0, 0.0).astype(BF16)
    da_hi, da_mid, da_lo = _split3(da)
    acs = _dot(tri, da_hi) + _dot(tri, da_mid) + _dot(tri, da_lo)
    a_last = acs[L - 1:L, :]
    d_end = jnp.exp(a_last - acs)
    d_start = jnp.exp(acs)
    cdec = jnp.broadcast_to(jnp.exp(a_last), (SUBLANES, LANES))

    ex = exp_ref[...]
    dt_e = _dot(dt.astype(BF16), ex)
    d_end_e = _dot(d_end.astype(BF16), ex)
    d_start_e = _dot_parts(_split2(d_start), ex)
    cdec_e = _dot_parts(_split2(cdec), ex)[0:1, :]

    xdt = xs * dt_e
    xb = xdt.astype(BF16)
    xd = (xdt * d_end_e).astype(BF16)
    acs_t = acs.T
    bm_t = bm.T
    bmb = bm.astype(BF16)
    cmb = cm.astype(BF16)

    for g in range(SSM_GROUPS):
        gs = slice(g * SSM_STATE, (g + 1) * SSM_STATE)
        ws = slice(g * SSM_GROUP_W, (g + 1) * SSM_GROUP_W)
        cb = _dot_nt(cmb[:, gs], bmb[:, gs])
        for r in range(SSM_HEADS // SSM_GROUPS):
            h = g * (SSM_HEADS // SSM_GROUPS) + r
            hs = slice(h * SSM_HEAD_DIM, (h + 1) * SSM_HEAD_DIM)
            diff = acs[:, h:h + 1] - acs_t[h:h + 1, :]
            lm = jnp.exp(jnp.where(causal, diff, -jnp.inf))
            y_ref[:, hs] = _dot((cb * lm).astype(BF16), xb[:, hs])
        st = state_ref[g]
        y_ref[:, ws] = y_ref[:, ws] + _dot(cmb[:, gs], st.astype(BF16)) * d_start_e[:, ws]
        state_ref[g] = cdec_e[:, ws] * st + _dot(bm_t[gs, :].astype(BF16), xd[:, ws])

    zz = z_ref[...]
    y = (y_ref[...] + dsk_ref[...] * xs) * _silu(zz)
    for g in range(SSM_GROUPS):
        ws = slice(g * SSM_GROUP_W, (g + 1) * SSM_GROUP_W)
        yg = y[:, ws]
        y_ref[:, ws] = yg * lax.rsqrt(jnp.mean(yg * yg, axis=-1, keepdims=True) + NORM_EPS)
    yn = (y_ref[...] * gn_ref[...]).astype(BF16)
    o_ref[...] = x_ref[...] + _dot(yn, wout_ref[...])


def _head_expand_matrix():
    e = np.zeros((LANES, SSM_D_INNER), np.float32)
    for h in range(SSM_HEADS):
        e[h, h * SSM_HEAD_DIM:(h + 1) * SSM_HEAD_DIM] = 1.0
    return jnp.asarray(e, dtype=BF16)


def _ssd(z, xbc, dt, x, batch, conv_w, conv_b, dt_bias, a_log, d_skip, g_norm, w_out):
    m, d = x.shape
    L = SSM_CHUNK
    nc = m // batch // L
    padh = lambda v: jnp.pad(v, (0, LANES - SSM_HEADS)).reshape(1, LANES)
    row = lambda n: pl.BlockSpec((L, n), lambda b, c: (b * nc + c, 0))
    return pl.pallas_call(
        _ssd_kernel,
        grid=(batch, nc),
        in_specs=[row(SSM_D_INNER), row(SSM_CONV_DIM), row(LANES), row(d),
                  _full((SSM_CONV, SSM_CONV_DIM)), _full((1, SSM_CONV_DIM)), _full((1, LANES)),
                  _full((1, LANES)), _full((1, SSM_D_INNER)), _full((1, SSM_D_INNER)),
                  _full((SSM_D_INNER, d)), _full((LANES, SSM_D_INNER))],
        out_specs=row(d),
        out_shape=jax.ShapeDtypeStruct((m, d), F32),
        scratch_shapes=[pltpu.VMEM((SSM_GROUPS, SSM_STATE, SSM_GROUP_W), F32),
                        pltpu.VMEM((L + 2 * SUBLANES, SSM_CONV_DIM), F32),
                        pltpu.VMEM((L, SSM_D_INNER), F32)],
        compiler_params=_params("parallel", "arbitrary"),
        name="ssm_scan",
    )(z, xbc, dt, x, conv_w.T, conv_b.reshape(1, -1), padh(dt_bias), padh(a_log),
      jnp.repeat(d_skip, SSM_HEAD_DIM).reshape(1, -1), g_norm.reshape(1, -1), w_out.astype(BF16),
      _head_expand_matrix())


def _router_kernel(r_ref, g_ref, rwh_ref, rwl_ref, rb_ref, xn_ref, meta_ref, cnt_ref, carry_ref):
    tm = r_ref.shape[0]
    i = pl.program_id(0)

    @pl.when(i == 0)
    def _():
        carry_ref[...] = jnp.zeros_like(carry_ref)

    xn = _rms_hat(r_ref[...]) * g_ref[...]
    xn_ref[...] = xn
    hb, hl = _split2(xn)
    logits = _dot(hb, rwh_ref[...]) + _dot(hl, rwh_ref[...]) + _dot(hb, rwl_ref[...]) + rb_ref[...]
    lane = lax.broadcasted_iota(jnp.int32, (tm, LANES), 1)
    vals, idxs = [], []
    cur = logits
    for _ in range(TOP_K):
        mx = jnp.max(cur, axis=-1, keepdims=True)
        ix = jnp.min(jnp.where(cur == mx, lane, LANES), axis=-1, keepdims=True)
        vals.append(mx)
        idxs.append(ix)
        cur = jnp.where(lane == ix, -jnp.inf, cur)
    es = [jnp.exp(v - vals[0]) for v in vals]
    den = es[0] + es[1] + es[2] + es[3]
    sel = jnp.zeros((tm, LANES), F32)
    for ix in idxs:
        sel = sel + jnp.where(lane == ix, 1.0, 0.0)
    row = lax.broadcasted_iota(jnp.int32, (tm, tm), 0)
    col = lax.broadcasted_iota(jnp.int32, (tm, tm), 1)
    strict = jnp.where(row > col, 1.0, 0.0).astype(BF16)
    before = _dot(strict, sel.astype(BF16)) + carry_ref[0:1, :]
    meta = jnp.zeros((tm, LANES), F32)
    for k in range(TOP_K):
        rank = jnp.sum(jnp.where(lane == idxs[k], before, 0.0), axis=-1, keepdims=True)
        meta = jnp.where(lane == k, idxs[k].astype(F32), meta)
        meta = jnp.where(lane == TOP_K + k, es[k] / den, meta)
        meta = jnp.where(lane == 2 * TOP_K + k, rank, meta)
    meta_ref[...] = meta
    carry = carry_ref[...] + jnp.sum(sel, axis=0, keepdims=True)
    carry_ref[...] = carry
    cnt_ref[...] = carry


def _router(r, g, r_w, r_b):
    m, d = r.shape
    tm = 512
    rw = jnp.pad(r_w, ((0, 0), (0, LANES - N_EXPERTS)))
    rwh, rwl = _split2(rw)
    rb = jnp.pad(r_b, (0, LANES - N_EXPERTS), constant_values=NEG).reshape(1, LANES)
    row = lambda n: pl.BlockSpec((tm, n), lambda i: (i, 0))
    return pl.pallas_call(
        _router_kernel,
        grid=(m // tm,),
        in_specs=[row(d), _full((1, d)), _full((d, LANES)), _full((d, LANES)), _full((1, LANES))],
        out_specs=[row(d), row(LANES), _full((SUBLANES, LANES))],
        out_shape=[jax.ShapeDtypeStruct((m, d), F32), jax.ShapeDtypeStruct((m, LANES), F32),
                   jax.ShapeDtypeStruct((SUBLANES, LANES), F32)],
        scratch_shapes=[pltpu.VMEM((SUBLANES, LANES), F32)],
        compiler_params=_params("arbitrary"),
        name="moe_router",
    )(r, g.reshape(1, d), rwh, rwl, rb)


def _for_each_row_copy(tm, row_copy):
    def start(tb, carry):
        base = pl.multiple_of(tb * SUBLANES, SUBLANES)
        for j in range(SUBLANES):
            for k in range(TOP_K):
                row_copy(base + j, k).start(priority=k % 2)
        return carry

    def wait(tb, carry):
        base = pl.multiple_of(tb * SUBLANES, SUBLANES)
        for j in range(SUBLANES):
            for k in range(TOP_K):
                row_copy(base + j, k).wait()
        return carry

    lax.fori_loop(0, tm // SUBLANES, start, 0)
    lax.fori_loop(0, tm // SUBLANES, wait, 0)


def _row_copies(tm, row_copy, start):
    def trip(tb, carry):
        base = pl.multiple_of(tb * SUBLANES, SUBLANES)
        for j in range(SUBLANES):
            for k in range(TOP_K):
                if start:
                    row_copy(base + j, k).start(priority=k % 2)
                else:
                    row_copy(base + j, k).wait()
        return carry

    lax.fori_loop(0, tm // SUBLANES, trip, 0)


def _dispatch_kernel(tail_ref, dest_ref, xn_ref, xs_ref, zeros_ref, sem, zsem):
    tm = xn_ref.shape[0]
    rows = zeros_ref.shape[0]

    def tail_copy(e):
        start = pl.multiple_of(jnp.maximum(tail_ref[e], 0), rows)
        return pltpu.make_async_copy(zeros_ref, xs_ref.at[pl.ds(start, rows), :], zsem)

    @pl.when(pl.program_id(0) == 0)
    def _():
        zeros_ref[...] = jnp.zeros_like(zeros_ref)
        for e in range(2 * N_EXPERTS):
            pl.when(tail_ref[e] >= 0)(tail_copy(e).start)
        for e in range(2 * N_EXPERTS):
            pl.when(tail_ref[e] >= 0)(tail_copy(e).wait)

    def row_copy(t, k):
        d = dest_ref[t * TOP_K + k]
        return pltpu.make_async_copy(xn_ref.at[pl.ds(t, 1), :], xs_ref.at[pl.ds(d, 1), :], sem)

    _for_each_row_copy(tm, row_copy)


def _dispatch(xn, dest_flat, tail_start, n_rows):
    m, d = xn.shape
    tm = 256
    grid_spec = pltpu.PrefetchScalarGridSpec(
        num_scalar_prefetch=1,
        grid=(m // tm,),
        in_specs=[pl.BlockSpec((tm * TOP_K,), lambda i, tail: (i,), memory_space=pltpu.SMEM),
                  pl.BlockSpec((tm, d), lambda i, tail: (i, 0))],
        out_specs=pl.BlockSpec(memory_space=pl.ANY),
        scratch_shapes=[pltpu.VMEM((EXPERT_ROWS, d), F32), pltpu.SemaphoreType.DMA(()),
                        pltpu.SemaphoreType.DMA(())],
    )
    return pl.pallas_call(
        _dispatch_kernel,
        grid_spec=grid_spec,
        out_shape=jax.ShapeDtypeStruct((n_rows, d), F32),
        compiler_params=_params("arbitrary"),
        name="moe_dispatch",
    )(tail_start, dest_flat, xn)


def _expert_kernel(be_ref, nv_ref, xs_ref, wgu_ref, bgu_ref, wdn_ref, bdn_ref, perm_ref, ys_ref,
                   wgu_s, wdn_s):
    b = pl.program_id(0)
    e = be_ref[b]
    prev = be_ref[jnp.maximum(b - 1, 0)]
    d_ff2 = wgu_s.shape[1]

    @pl.when(jnp.logical_or(b == 0, e != prev))
    def _():
        for c in range(d_ff2 // MXU_DIM):
            cs = slice(c * MXU_DIM, (c + 1) * MXU_DIM)
            wgu_s[:, cs] = _dot(wgu_ref[0, :, cs].astype(BF16), perm_ref[...]).astype(BF16)
        wdn_s[...] = wdn_ref[0].astype(BF16)

    @pl.when(nv_ref[b] > 0)
    def _():
        h = _dot(xs_ref[...].astype(BF16), wgu_s[...]) + bgu_ref[0]
        acts = []
        for c in range(d_ff2 // MXU_DIM):
            gate = jnp.minimum(h[:, c * MXU_DIM:c * MXU_DIM + LANES], SWIGLU_LIMIT)
            up = jnp.clip(h[:, c * MXU_DIM + LANES:(c + 1) * MXU_DIM], -SWIGLU_LIMIT, SWIGLU_LIMIT)
            acts.append((up + 1.0) * (gate * jax.nn.sigmoid(SWIGLU_ALPHA * gate)))
        a = jnp.concatenate(acts, axis=1).astype(BF16)
        ys_ref[...] = _dot(a, wdn_s[...]) + bdn_ref[0]

    @pl.when(nv_ref[b] == 0)
    def _():
        ys_ref[...] = jnp.zeros_like(ys_ref)


def _deinterleave_matrix():
    p = np.zeros((MXU_DIM, MXU_DIM), np.float32)
    for j in range(LANES):
        p[2 * j, j] = 1.0
        p[2 * j + 1, LANES + j] = 1.0
    return jnp.asarray(p, dtype=BF16)


def _experts(xs, blk_e, blk_nv, layer, w_gu_all, b_gu, w_dn_all, b_dn):
    n_rows, d = xs.shape
    _, n_e, _, d_ff2 = w_gu_all.shape
    d_ff = d_ff2 // 2
    tm = EXPERT_ROWS
    nt = d_ff2 // MXU_DIM
    w_gu = w_gu_all.reshape(-1, d, d_ff2)
    w_dn = w_dn_all.reshape(-1, d_ff, d)
    base = layer * n_e
    bgu_p = b_gu.reshape(n_e, nt, LANES, 2).transpose(0, 1, 3, 2).reshape(n_e, 1, d_ff2)
    grid_spec = pltpu.PrefetchScalarGridSpec(
        num_scalar_prefetch=2,
        grid=(n_rows // tm,),
        in_specs=[pl.BlockSpec((tm, d), lambda b, be, nv: (b, 0)),
                  pl.BlockSpec((1, d, d_ff2), lambda b, be, nv: (base + be[b], 0, 0)),
                  pl.BlockSpec((1, 1, d_ff2), lambda b, be, nv: (be[b], 0, 0)),
                  pl.BlockSpec((1, d_ff, d), lambda b, be, nv: (base + be[b], 0, 0)),
                  pl.BlockSpec((1, 1, d), lambda b, be, nv: (be[b], 0, 0)),
                  pl.BlockSpec((MXU_DIM, MXU_DIM), lambda b, be, nv: (0, 0))],
        out_specs=pl.BlockSpec((tm, d), lambda b, be, nv: (b, 0)),
        scratch_shapes=[pltpu.VMEM((d, d_ff2), BF16), pltpu.VMEM((d_ff, d), BF16)],
    )
    return pl.pallas_call(
        _expert_kernel,
        grid_spec=grid_spec,
        out_shape=jax.ShapeDtypeStruct((n_rows, d), F32),
        compiler_params=_params("arbitrary"),
        name="moe_experts",
    )(blk_e, blk_nv, xs, w_gu, bgu_p, w_dn, b_dn.reshape(n_e, 1, d), _deinterleave_matrix())


def _combine_kernel(dest_ref, dnext_ref, r_ref, meta_ref, p_ref, wple_ref, gple_ref, wgate_ref, ys_ref, o_ref,
                    buf, sems):
    tm = r_ref.shape[0]
    i = pl.program_id(0)
    n = pl.num_programs(0)

    def gather(dref, slot):
        def row_copy(t, k):
            d = dref[t * TOP_K + k]
            return pltpu.make_async_copy(ys_ref.at[pl.ds(d, 1), :], buf.at[slot, k, pl.ds(t, 1), :], sems.at[slot])
        return row_copy

    @pl.when(i == 0)
    def _():
        _row_copies(tm, gather(dest_ref, 0), start=True)

    for s in (0, 1):
        @pl.when(jnp.logical_and(i % 2 == s, i + 1 < n))
        def _(s=s):
            _row_copies(tm, gather(dnext_ref, 1 - s), start=True)

    for s in (0, 1):
        @pl.when(i % 2 == s)
        def _(s=s):
            _row_copies(tm, gather(dest_ref, s), start=False)

    acc = r_ref[...]
    meta = meta_ref[...]
    slot = i % 2
    for k in range(TOP_K):
        acc = acc + meta[:, TOP_K + k:TOP_K + k + 1] * buf[slot, k]
    gate = jax.nn.sigmoid(_dot((_rms_hat(acc) * gple_ref[...]).astype(BF16), wgate_ref[...]))
    o_ref[...] = acc + _dot(p_ref[...].astype(BF16), wple_ref[...]) * gate


def _combine(r, meta, dest_flat, ys, p, w_ple, g_ple, w_gate):
    m, d = r.shape
    pd = p.shape[1]
    tm = 256
    nb = m // tm
    row = lambda n: pl.BlockSpec((tm, n), lambda i: (i, 0))
    return pl.pallas_call(
        _combine_kernel,
        grid=(nb,),
        in_specs=[pl.BlockSpec((tm * TOP_K,), lambda i: (i,), memory_space=pltpu.SMEM),
                  pl.BlockSpec((tm * TOP_K,), lambda i: (jnp.minimum(i + 1, nb - 1),), memory_space=pltpu.SMEM),
                  row(d), row(LANES), row(pd), _full((pd, d)), _full((1, d)), _full((d, d)),
                  pl.BlockSpec(memory_space=pl.ANY)],
        out_specs=row(d),
        out_shape=jax.ShapeDtypeStruct((m, d), F32),
        scratch_shapes=[pltpu.VMEM((2, TOP_K, tm, d), F32), pltpu.SemaphoreType.DMA((2,))],
        compiler_params=_params("arbitrary"),
        name="moe_combine_ple",
    )(dest_flat, dest_flat, r, meta, p, w_ple.astype(BF16), g_ple.reshape(1, d), w_gate.astype(BF16), ys)


def _moe_ple(r, g_ffn, r_w, r_b, layer, w_gu, b_gu, w_dn, b_dn, p, w_ple, g_ple, w_gate):
    m, d = r.shape
    xn, meta, cnt = _router(r, g_ffn, r_w, r_b)
    top_i = meta[:, 0:TOP_K].astype(jnp.int32)
    rank = meta[:, 2 * TOP_K:3 * TOP_K].astype(jnp.int32)
    counts = cnt[0, :N_EXPERTS].astype(jnp.int32)
    padded = ((counts + EXPERT_ROWS - 1) // EXPERT_ROWS) * EXPERT_ROWS
    pad_end = jnp.cumsum(padded)
    pad_start = pad_end - padded
    dest = (pad_start[top_i] + rank).reshape(-1)
    n_rows = m * TOP_K + N_EXPERTS * EXPERT_ROWS
    n_blk = n_rows // EXPERT_ROWS
    blk_start = jnp.arange(n_blk, dtype=jnp.int32) * EXPERT_ROWS
    blk_e = jnp.sum((pad_end[None, :] <= blk_start[:, None]).astype(jnp.int32), axis=1)
    blk_e = jnp.minimum(blk_e, N_EXPERTS - 1)
    blk_nv = jnp.clip(counts[blk_e] - (blk_start - pad_start[blk_e]), 0, EXPERT_ROWS).astype(jnp.int32)
    unused = pad_end[-1] + jnp.arange(N_EXPERTS, dtype=jnp.int32) * EXPERT_ROWS
    tail_start = jnp.concatenate([jnp.where(padded > 0, pad_end - EXPERT_ROWS, -1),
                                  jnp.where(unused < n_rows, unused, -1)]).astype(jnp.int32)
    xs = _dispatch(xn, dest, tail_start, n_rows)
    ys = _experts(xs, blk_e, blk_nv, layer, w_gu, b_gu, w_dn, b_dn)
    return _combine(r, meta, dest, ys, p, w_ple, g_ple, w_gate)


def _seg_norm(x, bd, gain):
    w = x.shape[1]
    hi, lo = _split2(x * x)
    parts = []
    for c in range(w // MXU_DIM):
        cs = slice(c * MXU_DIM, (c + 1) * MXU_DIM)
        parts.append(_dot(hi[:, cs], bd) + _dot(lo[:, cs], bd))
    ss = parts[0] if len(parts) == 1 else jnp.concatenate(parts, axis=1)
    return x * lax.rsqrt(ss * (1.0 / ATT_HEAD_DIM) + NORM_EPS) * gain


def _rope64(x, cos, sin_signed):
    w = x.shape[1]
    half = ATT_HEAD_DIM // 2
    reps = w // LANES
    if reps > 1:
        cos = jnp.concatenate([cos] * reps, axis=1)
        sin_signed = jnp.concatenate([sin_signed] * reps, axis=1)
    lane = lax.broadcasted_iota(jnp.int32, x.shape, 1)
    first = (lane % ATT_HEAD_DIM) < half
    partner = jnp.where(first, pltpu.roll(x, w - half, axis=1), pltpu.roll(x, half, axis=1))
    return x * cos + partner * sin_signed


def _qkv_kernel(r_ref, gq_ref, gkv_ref, qw_ref, gw_ref, kvw_ref, qn_ref, kn_ref, cos_ref, sin_ref, bd_ref,
                qt_ref, gates_ref, kcv_ref, ks_ref, kw_ref, vst_ref, vwt_ref):
    xhat = _rms_hat(r_ref[...])
    hq = (xhat * gq_ref[...]).astype(BF16)
    hkv = (xhat * gkv_ref[...]).astype(BF16)
    cos = cos_ref[...]
    sin = sin_ref[...]
    bd = bd_ref[...]
    gw = ATT_KV_GROUPS * ATT_HEAD_DIM

    q = _dot(hq, qw_ref[...])
    q = _rope64(_seg_norm(q, bd, qn_ref[...]), cos, sin) * (ATT_HEAD_DIM ** -0.5 * LOG2E)
    qt_ref[0] = q.T.reshape(ATT_HEADS, ATT_HEAD_DIM, q.shape[0]).astype(BF16)
    gates_ref[...] = jax.nn.sigmoid(_dot(hq, gw_ref[...]))

    kv = _dot(hkv, kvw_ref[...])
    kcv_ref[...] = kv[:, 0:2 * gw]
    ks = _rope64(_seg_norm(kv[:, 2 * gw:3 * gw], bd, kn_ref[0:1, :]), cos, sin)
    kw = _rope64(_seg_norm(kv[:, 4 * gw:5 * gw], bd, kn_ref[1:2, :]), cos, sin)
    for g in range(ATT_KV_GROUPS):
        gs = slice(g * ATT_HEAD_DIM, (g + 1) * ATT_HEAD_DIM)
        ks_ref[0, g] = ks[:, gs].astype(BF16)
        kw_ref[0, g] = kw[:, gs].astype(BF16)
    vst = kv[:, 3 * gw:4 * gw].T.astype(BF16)
    vwt = kv[:, 5 * gw:6 * gw].T.astype(BF16)
    ones = jnp.ones((V_ROWS - ATT_HEAD_DIM, vst.shape[1]), BF16)
    for g in range(ATT_KV_GROUPS):
        gs = slice(g * ATT_HEAD_DIM, (g + 1) * ATT_HEAD_DIM)
        vst_ref[0, g, 0:ATT_HEAD_DIM, :] = vst[gs, :]
        vwt_ref[0, g, 0:ATT_HEAD_DIM, :] = vwt[gs, :]
        vst_ref[0, g, ATT_HEAD_DIM:V_ROWS, :] = ones
        vwt_ref[0, g, ATT_HEAD_DIM:V_ROWS, :] = ones


def _rope_tables(pos):
    dh = ATT_HEAD_DIM
    inv = 1.0 / (ROPE_THETA ** (jnp.arange(0, dh, 2, dtype=F32) / dh))
    ang = pos.astype(F32)[:, None] * inv
    c, s = jnp.cos(ang), jnp.sin(ang)
    cos = jnp.concatenate([c, c, c, c], axis=1)
    sin = jnp.concatenate([-s, s, -s, s], axis=1)
    return cos, sin


def _block_diag_ones():
    b = np.kron(np.eye(MXU_DIM // ATT_HEAD_DIM, dtype=np.float32),
                np.ones((ATT_HEAD_DIM, ATT_HEAD_DIM), np.float32))
    return jnp.asarray(b, dtype=BF16)


def _qkv(r, batch, g_q, g_kv, q_w, kv_w, q_norm, k_norm):
    m, d = r.shape
    seq = m // batch
    tm = 256
    nsb = seq // tm
    hd = ATT_HEADS * ATT_HEAD_DIM
    gw = ATT_KV_GROUPS * ATT_HEAD_DIM
    cos, sin = _rope_tables(jnp.arange(seq))
    qw = q_w[:, :hd].astype(BF16)
    gwt = jnp.pad(q_w[:, hd:], ((0, 0), (0, LANES - N_BRANCH * ATT_HEADS))).astype(BF16)
    qn = jnp.tile(q_norm, ATT_HEADS).reshape(1, hd)
    kn = jnp.stack([jnp.tile(k_norm[1], ATT_KV_GROUPS), jnp.tile(k_norm[2], ATT_KV_GROUPS)])
    row = lambda n: pl.BlockSpec((tm, n), lambda i: (i, 0))
    tab = pl.BlockSpec((tm, LANES), lambda i: (i % nsb, 0))
    vt = pl.BlockSpec((1, ATT_KV_GROUPS, V_ROWS, tm), lambda i: (i // nsb, 0, 0, i % nsb))
    kn_spec = pl.BlockSpec((1, ATT_KV_GROUPS, tm, ATT_HEAD_DIM), lambda i: (i // nsb, 0, i % nsb, 0))
    return pl.pallas_call(
        _qkv_kernel,
        grid=(m // tm,),
        in_specs=[row(d), _full((1, d)), _full((1, d)), _full((d, hd)), _full((d, LANES)),
                  _full((d, 6 * gw)), _full((1, hd)), _full((2, gw)), tab, tab,
                  _full((MXU_DIM, MXU_DIM))],
        out_specs=[pl.BlockSpec((1, ATT_HEADS, ATT_HEAD_DIM, tm), lambda i: (i // nsb, 0, 0, i % nsb)),
                   row(LANES), row(2 * gw), kn_spec, kn_spec, vt, vt],
        out_shape=[jax.ShapeDtypeStruct((batch, ATT_HEADS, ATT_HEAD_DIM, seq), BF16),
                   jax.ShapeDtypeStruct((m, LANES), F32),
                   jax.ShapeDtypeStruct((m, 2 * gw), F32),
                   jax.ShapeDtypeStruct((batch, ATT_KV_GROUPS, seq, ATT_HEAD_DIM), BF16),
                   jax.ShapeDtypeStruct((batch, ATT_KV_GROUPS, seq, ATT_HEAD_DIM), BF16),
                   jax.ShapeDtypeStruct((batch, ATT_KV_GROUPS, V_ROWS, seq), BF16),
                   jax.ShapeDtypeStruct((batch, ATT_KV_GROUPS, V_ROWS, seq), BF16)],
        compiler_params=_params("parallel"),
        name="nsa_qkv",
    )(r, g_q.reshape(1, d), g_kv.reshape(1, d), qw, gwt, kv_w.astype(BF16), qn, kn, cos, sin,
      _block_diag_ones())


def _cmp_kernel(x_ref, pe_ref, w1_ref, w2_ref, kn_ref, cos_ref, sin_ref, o_ref, ot_ref):
    nh = x_ref.shape[1]
    half_w = x_ref.shape[2]
    x = x_ref[0]
    h1 = _dot((x + pe_ref[0, 0:1, :]).astype(BF16), w1_ref[0, 0:half_w, :])
    h2 = _dot((x + pe_ref[0, 1:2, :]).astype(BF16), w1_ref[0, half_w:2 * half_w, :])
    pre = h1 + pltpu.roll(h2, nh - 1, axis=0)
    c = _dot(_silu(pre).astype(BF16), w2_ref[0])
    is_k = (pl.program_id(0) // ATT_KV_GROUPS) % 2 == 0
    cn = c * lax.rsqrt(jnp.sum(c * c, axis=-1, keepdims=True) * (1.0 / ATT_HEAD_DIM) + NORM_EPS) * kn_ref[...]
    cn = _rope64(cn, cos_ref[...], sin_ref[...])
    res = jnp.where(is_k, cn, c)
    rowi = lax.broadcasted_iota(jnp.int32, res.shape, 0)
    res = jnp.where(rowi < nh - 1, res, 0.0)
    o_ref[0] = res[:, 0:ATT_HEAD_DIM].astype(BF16)
    ot_ref[0] = res.T[0:ATT_HEAD_DIM, :].astype(BF16)


def _compress(kcv, batch, cmp_pe, cmp_w1, cmp_w2, k_norm0):
    m = kcv.shape[0]
    seq = m // batch
    nh = seq // CMP_STRIDE
    dh = ATT_HEAD_DIM
    half_w = CMP_STRIDE * dh
    hidden = cmp_w1.shape[2]
    x = kcv.reshape(batch, seq, 2, ATT_KV_GROUPS, dh).transpose(0, 2, 3, 1, 4)
    x = x.reshape(batch * 2 * ATT_KV_GROUPS, nh, half_w)
    pe = cmp_pe.reshape(2, 2, half_w)
    w2 = jnp.pad(cmp_w2, ((0, 0), (0, 0), (0, LANES - dh))).astype(BF16)
    kn = jnp.pad(k_norm0, (0, LANES - dh)).reshape(1, LANES)
    cos, sin = _rope_tables(jnp.arange(nh) * CMP_STRIDE + (CMP_BLOCK - 1))
    nidx = lambda i: (i // ATT_KV_GROUPS) % 2
    n_all = batch * 2 * ATT_KV_GROUPS
    c, ct = pl.pallas_call(
        _cmp_kernel,
        grid=(n_all,),
        in_specs=[pl.BlockSpec((1, nh, half_w), lambda i: (i, 0, 0)),
                  pl.BlockSpec((1, 2, half_w), lambda i: (nidx(i), 0, 0)),
                  pl.BlockSpec((1, 2 * half_w, hidden), lambda i: (nidx(i), 0, 0)),
                  pl.BlockSpec((1, hidden, LANES), lambda i: (nidx(i), 0, 0)),
                  _full((1, LANES)), _full((nh, LANES)), _full((nh, LANES))],
        out_specs=[pl.BlockSpec((1, nh, dh), lambda i: (i, 0, 0)),
                   pl.BlockSpec((1, dh, nh), lambda i: (i, 0, 0))],
        out_shape=[jax.ShapeDtypeStruct((n_all, nh, dh), BF16),
                   jax.ShapeDtypeStruct((n_all, dh, nh), BF16)],
        compiler_params=_params("parallel"),
        name="nsa_compress",
    )(x, pe, cmp_w1.astype(BF16), w2, kn, cos, sin)
    return (c.reshape(batch, 2, ATT_KV_GROUPS, nh, dh), ct.reshape(batch, 2, ATT_KV_GROUPS, dh, nh))


def _attn_kernel(qt_ref, gates_ref, kc_ref, vct_ref, ks_ref, vst_ref, kw_ref, vwt_ref, selwt_ref,
                 o_ref, s_ref, ps_ref, sel_ref, m_ref, acc_ref, oc_ref, ow_ref,
                 sa_ref, sb_ref, mxa_ref, mxb_ref, sw_ref):
    i = pl.program_id(2)
    n_cmp = kc_ref.shape[-2]
    n_sel = selwt_ref.shape[0]
    t0 = i * Q_BLOCK
    t_row = t0 + lax.broadcasted_iota(jnp.int32, (1, Q_BLOCK), 1)
    heads = range(ATT_REP)

    k_c = kc_ref[0, 0, 0]
    vt_c = vct_ref[0, 0, 0]
    cend = lax.broadcasted_iota(jnp.int32, (n_cmp, 1), 0) * CMP_STRIDE + (CMP_BLOCK - 1)
    ok_c = cend <= t_row
    q_all = jnp.concatenate([qt_ref[0, r] for r in heads], axis=1)
    s_c = _dot(k_c, q_all)
    for r in heads:
        s_ref[r, 0:n_cmp, :] = jnp.where(ok_c, s_c[:, r * Q_BLOCK:(r + 1) * Q_BLOCK], NEG)
    for r in heads:
        s = s_ref[r, 0:n_cmp, :]
        p = jnp.where(ok_c, jnp.exp2(s - jnp.max(s, axis=0, keepdims=True)), 0.0)
        den = jnp.sum(p, axis=0, keepdims=True)
        pc = p * (1.0 / jnp.where(den > 0, den, 1.0))
        oc_ref[r] = _dot(vt_c, pc.astype(BF16))
        if r == 0:
            ps_ref[...] = pc
        else:
            ps_ref[...] = ps_ref[...] + pc

    ps_hi, ps_mid, ps_lo = _split3(ps_ref[...])
    selwt = selwt_ref[...]
    imp = _dot(selwt, ps_hi) + _dot(selwt, ps_mid) + _dot(selwt, ps_lo)
    j = lax.broadcasted_iota(jnp.int32, (n_sel, Q_BLOCK), 0)
    jf = j.astype(F32)
    cur = t_row // SEL_BLOCK
    valid = j <= cur
    forced = (j == 0) | (j == cur) | (j == cur - 1)
    sc = jnp.where(forced, -jnp.inf, jnp.where(valid, imp, -1.0))
    sel = jnp.where(forced, 1.0, 0.0)
    for _ in range(min(SEL_TOPK, n_sel) - N_FORCED):
        mx = jnp.max(sc, axis=0, keepdims=True)
        ixf = jnp.min(jnp.where(sc == mx, jf, float(n_sel)), axis=0, keepdims=True)
        pick = jf == ixf
        sel = jnp.where(pick, 1.0, sel)
        sc = jnp.where(pick, -jnp.inf, sc)

    wk = WINDOW + Q_BLOCK
    st = pl.multiple_of(jnp.maximum(t0 - WINDOW, 0), LANES)
    k_w = kw_ref[0, 0, pl.ds(st, wk), :]
    vt_w = vwt_ref[0, 0, :, pl.ds(st, wk)]
    kpos_w = st + lax.broadcasted_iota(jnp.int32, (wk, 1), 0)
    ok_w = (kpos_w <= t_row) & (kpos_w > t_row - WINDOW)
    s_w = _dot(k_w, q_all)
    for r in heads:
        sw_ref[r] = jnp.where(ok_w, s_w[:, r * Q_BLOCK:(r + 1) * Q_BLOCK], NEG)
    for r0 in range(0, ATT_REP, 2):
        ps = []
        for r in (r0, r0 + 1):
            s = sw_ref[r]
            ps.append(jnp.exp2(s - jnp.max(s, axis=0, keepdims=True)).astype(BF16))
        res = _dot(vt_w, jnp.concatenate(ps, axis=1))
        for n, r in enumerate((r0, r0 + 1)):
            rs = res[:, n * Q_BLOCK:(n + 1) * Q_BLOCK]
            ow_ref[r] = rs[0:ATT_HEAD_DIM] / rs[ATT_HEAD_DIM:ATT_HEAD_DIM + 1]

    sel_ref[...] = jnp.where(valid, sel, 0.0)

    blocks = SEL_CHUNK // SEL_BLOCK
    last_chunk = ks_ref.shape[2] // SEL_CHUNK - 1
    n_chunks = t0 // SEL_CHUNK
    off_d = pl.multiple_of(n_chunks * SEL_CHUNK, SEL_CHUNK)
    picked_d = sel_ref[pl.ds(pl.multiple_of(n_chunks * blocks, blocks), blocks), :]
    keep_d = jnp.broadcast_to(picked_d[:, None, :], (blocks, SEL_BLOCK, Q_BLOCK)).reshape(SEL_CHUNK, Q_BLOCK)
    kpos_d = off_d + lax.broadcasted_iota(jnp.int32, (SEL_CHUNK, 1), 0)
    mk_d = jnp.where(kpos_d <= t_row, keep_d, 0.0) > 0.5
    s_d = _dot(ks_ref[0, 0, pl.ds(off_d, SEL_CHUNK), :], q_all)
    vt_d = vst_ref[0, 0, :, pl.ds(off_d, SEL_CHUNK)]
    for r0 in range(0, ATT_REP, 2):
        ps = []
        for r in (r0, r0 + 1):
            s = jnp.where(mk_d, s_d[:, r * Q_BLOCK:(r + 1) * Q_BLOCK], NEG)
            mx = jnp.max(s, axis=0, keepdims=True)
            m_ref[r:r + 1, :] = mx
            ps.append(jnp.exp2(s - mx).astype(BF16))
        res = _dot(vt_d, jnp.concatenate(ps, axis=1))
        for n, r in enumerate((r0, r0 + 1)):
            acc_ref[r] = res[:, n * Q_BLOCK:(n + 1) * Q_BLOCK]


    def scores(c, buf_ref, mx_ref):
        cc = jnp.minimum(c, last_chunk)
        off = pl.multiple_of(cc * SEL_CHUNK, SEL_CHUNK)
        k_s = ks_ref[0, 0, pl.ds(off, SEL_CHUNK), :]
        picked = sel_ref[pl.ds(pl.multiple_of(cc * blocks, blocks), blocks), :]
        mk = jnp.broadcast_to(picked[:, None, :], (blocks, SEL_BLOCK, Q_BLOCK)).reshape(SEL_CHUNK, Q_BLOCK) > 0.5
        s_all = _dot(k_s, q_all)
        for r in heads:
            s = jnp.where(mk, s_all[:, r * Q_BLOCK:(r + 1) * Q_BLOCK], NEG)
            buf_ref[r] = s
            mx_ref[r:r + 1, :] = jnp.max(s, axis=0, keepdims=True)

    def accumulate(c, buf_ref, mx_ref):
        cc = jnp.minimum(c, last_chunk)
        off = pl.multiple_of(cc * SEL_CHUNK, SEL_CHUNK)
        vt_s = vst_ref[0, 0, :, pl.ds(off, SEL_CHUNK)]
        for r0 in range(0, ATT_REP, 2):
            ps, alphas = [], []
            for r in (r0, r0 + 1):
                m_old = m_ref[r:r + 1, :]
                m_new = jnp.maximum(m_old, mx_ref[r:r + 1, :])
                ps.append(jnp.exp2(buf_ref[r] - m_new).astype(BF16))
                alphas.append(jnp.exp2(m_old - m_new))
                m_ref[r:r + 1, :] = m_new
            res = _dot(vt_s, jnp.concatenate(ps, axis=1))
            for n, r in enumerate((r0, r0 + 1)):
                acc_ref[r] = alphas[n] * acc_ref[r] + res[:, n * Q_BLOCK:(n + 1) * Q_BLOCK]

    scores(0, sa_ref, mxa_ref)

    def body(k, carry):
        scores(2 * k + 1, sb_ref, mxb_ref)
        accumulate(2 * k, sa_ref, mxa_ref)
        scores(2 * k + 2, sa_ref, mxa_ref)
        accumulate(2 * k + 1, sb_ref, mxb_ref)
        return carry

    lax.fori_loop(0, n_chunks // 2, body, 0)

    @pl.when(n_chunks % 2 == 1)
    def _():
        accumulate(n_chunks - 1, sa_ref, mxa_ref)

    gts = gates_ref[0, 0, 0]
    outs = []
    for r in heads:
        g = [gts[r * N_BRANCH + b:r * N_BRANCH + b + 1, :] for b in range(N_BRANCH)]
        o_sel = acc_ref[r, 0:ATT_HEAD_DIM, :] / acc_ref[r, ATT_HEAD_DIM:ATT_HEAD_DIM + 1, :]
        outs.append(g[0] * oc_ref[r] + g[1] * o_sel + g[2] * ow_ref[r])
    for pr in range(ATT_REP // 2):
        pair = jnp.concatenate([outs[2 * pr], outs[2 * pr + 1]], axis=0)
        o_ref[0, :, pr * LANES:(pr + 1) * LANES] = pair.T


def _cmp_to_sel_weights(seq):
    n_cmp = (seq - CMP_BLOCK) // CMP_STRIDE + 1
    n_sel = seq // SEL_BLOCK
    cs = np.arange(n_cmp) * CMP_STRIDE
    ss = np.arange(n_sel) * SEL_BLOCK
    ov = np.minimum(cs[:, None] + CMP_BLOCK, ss[None, :] + SEL_BLOCK) - np.maximum(cs[:, None], ss[None, :])
    w = np.clip(ov, 0, None).astype(np.float32) / CMP_BLOCK
    w = np.concatenate([w, np.zeros((seq // CMP_STRIDE - n_cmp, n_sel), np.float32)], axis=0)
    return jnp.asarray(w.T, dtype=BF16)


def _attention(qt, gates, kc, kct, ks, kw, vst, vwt, batch):
    seq = qt.shape[3]
    nqb = seq // Q_BLOCK
    dh = ATT_HEAD_DIM
    n_cmp = seq // CMP_STRIDE
    n_sel = seq // SEL_BLOCK
    hd = ATT_HEADS * dh
    gw = ATT_KV_GROUPS * dh
    wk = WINDOW + Q_BLOCK
    g3 = gates[:, :N_BRANCH * ATT_HEADS].reshape(batch, nqb, Q_BLOCK, ATT_KV_GROUPS, ATT_REP * N_BRANCH)
    g3 = g3.transpose(0, 3, 1, 4, 2)
    g3 = jnp.pad(g3, ((0, 0), (0, 0), (0, 0), (0, GATE_ROWS - ATT_REP * N_BRANCH), (0, 0)))
    per_bg = lambda *shape: pl.BlockSpec((1, 1) + shape, lambda b, g, i: (b, g, 0, 0))
    head_tile = pltpu.VMEM((ATT_REP, dh, Q_BLOCK), F32)
    return pl.pallas_call(
        _attn_kernel,
        grid=(batch, ATT_KV_GROUPS, nqb),
        in_specs=[pl.BlockSpec((1, ATT_REP, dh, Q_BLOCK), lambda b, g, i: (b, g, 0, i)),
                  pl.BlockSpec((1, 1, 1, GATE_ROWS, Q_BLOCK), lambda b, g, i: (b, g, i, 0, 0)),
                  pl.BlockSpec((1, 1, 1, n_cmp, dh), lambda b, g, i: (b, 0, g, 0, 0)),
                  pl.BlockSpec((1, 1, 1, dh, n_cmp), lambda b, g, i: (b, 1, g, 0, 0)),
                  per_bg(seq, dh), per_bg(V_ROWS, seq), per_bg(seq, dh), per_bg(V_ROWS, seq),
                  pl.BlockSpec((n_sel, n_cmp), lambda b, g, i: (0, 0))],
        out_specs=pl.BlockSpec((1, Q_BLOCK, gw), lambda b, g, i: (b, i, g)),
        out_shape=jax.ShapeDtypeStruct((batch, seq, hd), F32),
        scratch_shapes=[pltpu.VMEM((ATT_REP, n_cmp, Q_BLOCK), F32),
                        pltpu.VMEM((n_cmp, Q_BLOCK), F32), pltpu.VMEM((n_sel, Q_BLOCK), F32),
                        pltpu.VMEM((SUBLANES, Q_BLOCK), F32),
                        pltpu.VMEM((ATT_REP, V_ROWS, Q_BLOCK), F32), head_tile, head_tile,
                        pltpu.VMEM((ATT_REP, SEL_CHUNK, Q_BLOCK), F32), pltpu.VMEM((ATT_REP, SEL_CHUNK, Q_BLOCK), F32),
                        pltpu.VMEM((SUBLANES, Q_BLOCK), F32), pltpu.VMEM((SUBLANES, Q_BLOCK), F32),
                        pltpu.VMEM((ATT_REP, wk, Q_BLOCK), F32)],
        compiler_params=_params("parallel", "parallel", "arbitrary"),
        name="nsa_attention",
    )(qt, g3, kc, kct, ks, vst, kw, vwt, _cmp_to_sel_weights(seq))


def _oproj_kernel(a_ref, r_ref, w_ref, o_ref):
    o_ref[...] = r_ref[...] + _dot(a_ref[...].astype(BF16), w_ref[...])


def _oproj(a, r, w):
    m, d = r.shape
    k = a.shape[1]
    tm = 512
    row = lambda n: pl.BlockSpec((tm, n), lambda i: (i, 0))
    return pl.pallas_call(
        _oproj_kernel,
        grid=(m // tm,),
        in_specs=[row(k), row(d), _full((k, d))],
        out_specs=row(d),
        out_shape=jax.ShapeDtypeStruct((m, d), F32),
        compiler_params=_params("parallel"),
        name="nsa_oproj",
    )(a, r, w.astype(BF16))


def kernel(x, p, g_mix, g_ffn, m_w_in, m_conv_w, m_conv_b, m_dt_bias, m_a_log, m_d, m_g_norm, m_w_out,
           kv_g, kv_w, cmp_pe, cmp_w1, cmp_w2, k_norm, q_w, q_norm, o_w,
           r_w, r_b, e_w_gu, e_b_gu, e_w_dn, e_b_dn, ple_w, ple_g, ple_gate_w):
    batch, seq, d = x.shape
    m = batch * seq
    depth = p.shape[0]
    n_a = m_w_in.shape[0]
    r = x.reshape(m, d)
    pf = p.reshape(depth, m, p.shape[-1])
    shared = None
    for i in range(depth):
        if i < n_a:
            z, xbc, dt = _inproj(r, g_mix[i], m_w_in[i])
            r = _ssd(z, xbc, dt, r, batch, m_conv_w[i], m_conv_b[i], m_dt_bias[i], m_a_log[i], m_d[i],
                     m_g_norm[i], m_w_out[i])
        else:
            jb = i - n_a
            qt, gates, kcv, ks, kw, vst, vwt = _qkv(r, batch, g_mix[i], kv_g, q_w[jb], kv_w, q_norm[jb], k_norm)
            if shared is None:
                shared = _compress(kcv, batch, cmp_pe, cmp_w1, cmp_w2, k_norm[0]) + (ks, kw, vst, vwt)
            a = _attention(qt, gates, *shared, batch)
            r = _oproj(a.reshape(m, -1), r, o_w[jb])
        r = _moe_ple(r, g_ffn[i], r_w[i], r_b[i], i, e_w_gu, e_b_gu[i], e_w_dn, e_b_dn[i],
                     pf[i], ple_w[i], ple_g[i], ple_gate_w[i])
    return r.reshape(batch, seq, d)
```
